```python
import math
import jax, jax.numpy as jnp
from jax import lax
import numpy as np

D_MODEL = 1024
BATCH = 8
SEQ = 2048
DEPTH = 4
DEC_BATCH = 128
DEC_SEQ = 8
PAST_LEN = 2048
PAGE_SIZE = 128

N_META = 16
N_MIXERS = 2
N_FOX_LAYERS = (DEPTH + 1) // 2
N_SSD_LAYERS = DEPTH // 2
FOX_HEADS = 16
FOX_HEAD_DIM = D_MODEL // FOX_HEADS
Q_BLOCK = 128
FORGET_BIAS_INIT = 3.0
SSD_EXPAND = 2
SSD_D_INNER = SSD_EXPAND * D_MODEL
SSD_HEAD_DIM = 64
SSD_HEADS = SSD_D_INNER // SSD_HEAD_DIM
SSD_GROUPS = 4
SSD_HPG = SSD_HEADS // SSD_GROUPS
SSD_STATE = 128
SSD_CONV = 4
SSD_CONV_DIM = SSD_D_INNER + 2 * SSD_GROUPS * SSD_STATE
SSD_CHUNK = 128
SSD_IN_DIM = SSD_D_INNER + SSD_CONV_DIM + SSD_HEADS
FFN_HIDDEN = ((-(-8 * D_MODEL // 3) + 255) // 256) * 256
FOX_IN_DIM = 3 * D_MODEL + FOX_HEADS
RMS_EPS = 1e-6
NEG_BIG = -1e30

kernel_name = "fox_mamba2_hybrid_step"


def rmsnorm(x, g):
    xf = x.astype(jnp.float32)
    y = xf * lax.rsqrt(jnp.mean(xf * xf, axis=-1, keepdims=True) + RMS_EPS)
    return (y * g.astype(jnp.float32)).astype(x.dtype)


def swiglu(x, w_gate, w_up, w_down):
    return (jax.nn.silu(x @ w_gate) * (x @ w_up)) @ w_down


def fox_project(h, w_in, b_f):
    b, l, _ = h.shape
    q, k, v, f_logit = jnp.split(h @ w_in, [D_MODEL, 2 * D_MODEL, 3 * D_MODEL], axis=-1)
    shp = (b, l, FOX_HEADS, FOX_HEAD_DIM)
    log_f = jax.nn.log_sigmoid((f_logit + b_f).astype(jnp.float32))
    return q.reshape(shp), k.reshape(shp), v.reshape(shp), log_f


def fox_attend(q, q_pos, F_q, k, v, k_pos, F_k):
    s = jnp.einsum('bqhd,bkhd->bhqk', q, k).astype(jnp.float32) * (FOX_HEAD_DIM ** -0.5)
    bias = jnp.swapaxes(F_q, 1, 2)[..., :, None] - jnp.swapaxes(F_k, 1, 2)[..., None, :]
    causal = k_pos[None, :] <= q_pos[:, None]
    p = jax.nn.softmax(jnp.where(causal, s + bias, NEG_BIG), axis=-1)
    return jnp.einsum('bhqk,bkhd->bqhd', p.astype(v.dtype), v)


def fox_prompt(h, w_in, b_f, w_out):
    b, T, _ = h.shape
    q, k, v, log_f = fox_project(h, w_in, b_f)
    F = jnp.cumsum(log_f, axis=1)
    pos = jnp.arange(T)
    o_meta = fox_attend(q[:, :N_META], pos[:N_META], F[:, :N_META],
                        k[:, :N_META], v[:, :N_META], pos[:N_META], F[:, :N_META])
    n_blk = (T - N_META) // Q_BLOCK

    def block(i):
        start = N_META + i * Q_BLOCK
        qb = lax.dynamic_slice_in_dim(q, start, Q_BLOCK, axis=1)
        Fb = lax.dynamic_slice_in_dim(F, start, Q_BLOCK, axis=1)
        return fox_attend(qb, start + jnp.arange(Q_BLOCK), Fb, k, v, pos, F)

    o_real = lax.map(block, jnp.arange(n_blk))
    o_real = jnp.moveaxis(o_real, 0, 1).reshape(b, T - N_META, FOX_HEADS, FOX_HEAD_DIM)
    o = jnp.concatenate([o_meta, o_real], axis=1).reshape(b, T, D_MODEL)
    return o @ w_out, k, v, log_f


def fox_sample(h, w_in, b_f, w_out, ck, cv, clf, page_table):
    b, L, _ = h.shape
    q, k, v, log_f = fox_project(h, w_in, b_f)
    P = page_table.shape[1] * PAGE_SIZE
    k_past = ck[page_table].reshape(b, P, FOX_HEADS, FOX_HEAD_DIM).astype(k.dtype)
    v_past = cv[page_table].reshape(b, P, FOX_HEADS, FOX_HEAD_DIM).astype(v.dtype)
    lf_past = clf[page_table].reshape(b, P, FOX_HEADS).astype(jnp.float32)
    k_all = jnp.concatenate([k_past, k], axis=1)
    v_all = jnp.concatenate([v_past, v], axis=1)
    F = jnp.cumsum(jnp.concatenate([lf_past, log_f], axis=1), axis=1)
    pos = jnp.arange(P + L)
    o = fox_attend(q, pos[P:], F[:, P:], k_all, v_all, pos, F).reshape(b, L, D_MODEL)
    return o @ w_out, k, v, log_f


def ssd_scan(x, dt, A, B, C, h0, chunk):
    b, l = x.shape[:2]
    c = l // chunk

    def chunks(t):
        return t.reshape((b, c, chunk) + t.shape[2:])

    x, dt, B, C = chunks(x), chunks(dt), chunks(B), chunks(C)
    a_cum = jnp.cumsum(dt * A, axis=2)
    xdt = x * dt[..., None]
    idx = jnp.arange(chunk)
    causal = (idx[:, None] >= idx[None, :])[:, :, None, None]
    diff = a_cum[:, :, :, None] - a_cum[:, :, None]
    Lmat = jnp.exp(jnp.where(causal, diff, -jnp.inf))
    cb = jnp.einsum('bcign,bcjgn->bcijg', C, B)
    y_diag = jnp.einsum('bcijg,bcijgr,bcjgrp->bcigrp', cb, Lmat, xdt)
    decay_end = jnp.exp(a_cum[:, :, -1:] - a_cum)
    states = jnp.einsum('bcjgn,bcjgr,bcjgrp->bcgrpn', B, decay_end, xdt)
    chunk_decay = jnp.exp(a_cum[:, :, -1])

    def step(h, inp):
        s_c, d_c = inp
        return d_c[..., None, None] * h + s_c, h

    h_last, h_prev = lax.scan(step, h0, (jnp.moveaxis(states, 1, 0), jnp.moveaxis(chunk_decay, 1, 0)))
    h_prev = jnp.moveaxis(h_prev, 0, 1)
    y_off = jnp.einsum('bcign,bcigr,bcgrpn->bcigrp', C, jnp.exp(a_cum), h_prev)
    y = (y_diag + y_off).reshape((b, l) + y_diag.shape[3:])
    return y, h_last


def ssd_mixer(h, conv_ctx, h0, segments, w_in, conv_w, conv_b, dt_bias, a_log, d_skip, norm_w, w_out):
    b, L, _ = h.shape
    f32 = jnp.float32
    z, xbc, dt_raw = jnp.split(h @ w_in, [SSD_D_INNER, SSD_D_INNER + SSD_CONV_DIM], axis=-1)
    xbc_ext = jnp.concatenate([conv_ctx.astype(xbc.dtype), xbc], axis=1)
    new_conv = xbc_ext[:, -(SSD_CONV - 1):]
    xbc_c = lax.conv_general_dilated(xbc_ext, conv_w[:, None, :].astype(xbc.dtype), (1,), 'VALID',
                                     dimension_numbers=('NWC', 'WIO', 'NWC'),
                                     feature_group_count=SSD_CONV_DIM)
    xbc_c = jax.nn.silu((xbc_c + conv_b).astype(f32))
    xs, Bm, Cm = jnp.split(xbc_c, [SSD_D_INNER, SSD_D_INNER + SSD_GROUPS * SSD_STATE], axis=-1)
    xs = xs.reshape(b, L, SSD_GROUPS, SSD_HPG, SSD_HEAD_DIM)
    Bm = Bm.reshape(b, L, SSD_GROUPS, SSD_STATE)
    Cm = Cm.reshape(b, L, SSD_GROUPS, SSD_STATE)
    dt = jax.nn.softplus(dt_raw.astype(f32) + dt_bias.astype(f32)).reshape(b, L, SSD_GROUPS, SSD_HPG)
    A = -jnp.exp(a_log.astype(f32)).reshape(SSD_GROUPS, SSD_HPG)
    hs = h0.astype(f32).reshape(b, SSD_GROUPS, SSD_HPG, SSD_HEAD_DIM, SSD_STATE)
    ys = []
    start = 0
    for length, chunk in segments:
        sl = slice(start, start + length)
        y_seg, hs = ssd_scan(xs[:, sl], dt[:, sl], A, Bm[:, sl], Cm[:, sl], hs, chunk)
        ys.append(y_seg)
        start += length
    y = jnp.concatenate(ys, axis=1) + d_skip.astype(f32).reshape(SSD_GROUPS, SSD_HPG)[..., None] * xs
    g = y.reshape(b, L, SSD_GROUPS, -1) * jax.nn.silu(z.astype(f32)).reshape(b, L, SSD_GROUPS, -1)
    g = g * lax.rsqrt(jnp.mean(g * g, axis=-1, keepdims=True) + RMS_EPS)
    g = g.reshape(b, L, SSD_D_INNER) * norm_w.astype(f32)
    out = g.astype(h.dtype) @ w_out
    return out, hs.reshape(b, SSD_HEADS, SSD_HEAD_DIM, SSD_STATE), new_conv


def setup_inputs(seed: int = 0) -> dict:
    key = jax.random.key(seed)
    ks = jax.random.split(key, 32)
    n = jax.random.normal
    n_pages = PAST_LEN // PAGE_SIZE
    n_used = DEC_BATCH * n_pages
    n_phys = n_used + max(1, n_used // 4)
    page_table = jax.random.permutation(ks[0], n_phys)[:n_used].reshape(DEC_BATCH, n_pages).astype(jnp.int32)
    dt0 = jnp.exp(jax.random.uniform(ks[1], (N_SSD_LAYERS, SSD_HEADS)) * (math.log(0.1) - math.log(0.001)) + math.log(0.001))
    return {
        "x_prompt": n(ks[2], (BATCH, SEQ, D_MODEL), jnp.float32),
        "x_sample": n(ks[3], (DEC_BATCH, DEC_SEQ, D_MODEL), jnp.float32),
        "cache_k": n(ks[4], (N_FOX_LAYERS, n_phys, PAGE_SIZE, FOX_HEADS, FOX_HEAD_DIM), jnp.float32),
        "cache_v": n(ks[5], (N_FOX_LAYERS, n_phys, PAGE_SIZE, FOX_HEADS, FOX_HEAD_DIM), jnp.float32),
        "cache_lf": jax.nn.log_sigmoid(n(ks[6], (N_FOX_LAYERS, n_phys, PAGE_SIZE, FOX_HEADS), jnp.float32) + FORGET_BIAS_INIT),
        "state_ssm": 0.5 * n(ks[7], (N_SSD_LAYERS, DEC_BATCH, SSD_HEADS, SSD_HEAD_DIM, SSD_STATE), jnp.float32),
        "state_conv": n(ks[8], (N_SSD_LAYERS, DEC_BATCH, SSD_CONV - 1, SSD_CONV_DIM), jnp.float32),
        "page_table": page_table,
        "meta_tokens": n(ks[9], (N_META, D_MODEL), jnp.float32),
        "norm_mix": 1.0 + 0.02 * n(ks[10], (DEPTH, D_MODEL), jnp.float32),
        "norm_ffn": 1.0 + 0.02 * n(ks[11], (DEPTH, D_MODEL), jnp.float32),
        "norm_final": 1.0 + 0.02 * n(ks[12], (D_MODEL,), jnp.float32),
        "fox_w_in": n(ks[13], (N_FOX_LAYERS, D_MODEL, FOX_IN_DIM), jnp.float32) * D_MODEL ** -0.5,
        "fox_b_f": FORGET_BIAS_INIT + 0.1 * n(ks[14], (N_FOX_LAYERS, FOX_HEADS), jnp.float32),
        "fox_w_out": n(ks[15], (N_FOX_LAYERS, D_MODEL, D_MODEL), jnp.float32) * D_MODEL ** -0.5,
        "ssd_w_in": n(ks[16], (N_SSD_LAYERS, D_MODEL, SSD_IN_DIM), jnp.float32) * D_MODEL ** -0.5,
        "ssd_conv_w": n(ks[17], (N_SSD_LAYERS, SSD_CONV, SSD_CONV_DIM), jnp.float32) * SSD_CONV ** -0.5,
        "ssd_conv_b": 0.01 * n(ks[18], (N_SSD_LAYERS, SSD_CONV_DIM), jnp.float32),
        "ssd_dt_bias": dt0 + jnp.log(-jnp.expm1(-dt0)),
        "ssd_a_log": jnp.log(jax.random.uniform(ks[19], (N_SSD_LAYERS, SSD_HEADS), minval=1.0, maxval=16.0)),
        "ssd_d": 1.0 + 0.1 * n(ks[20], (N_SSD_LAYERS, SSD_HEADS), jnp.float32),
        "ssd_norm": 1.0 + 0.02 * n(ks[21], (N_SSD_LAYERS, SSD_D_INNER), jnp.float32),
        "ssd_w_out": n(ks[22], (N_SSD_LAYERS, SSD_D_INNER, D_MODEL), jnp.float32) * SSD_D_INNER ** -0.5,
        "ffn_w_gate": n(ks[23], (DEPTH, D_MODEL, FFN_HIDDEN), jnp.float32) * D_MODEL ** -0.5,
        "ffn_w_up": n(ks[24], (DEPTH, D_MODEL, FFN_HIDDEN), jnp.float32) * D_MODEL ** -0.5,
        "ffn_w_down": n(ks[25], (DEPTH, FFN_HIDDEN, D_MODEL), jnp.float32) * FFN_HIDDEN ** -0.5,
    }


def reference(x_prompt, x_sample, cache_k, cache_v, cache_lf, state_ssm, state_conv, page_table,
              meta_tokens, norm_mix, norm_ffn, norm_final, fox_w_in, fox_b_f, fox_w_out,
              ssd_w_in, ssd_conv_w, ssd_conv_b, ssd_dt_bias, ssd_a_log, ssd_d, ssd_norm, ssd_w_out,
              ffn_w_gate, ffn_w_up, ffn_w_down):
    bp, seq, _ = x_prompt.shape
    bs, ls, _ = x_sample.shape
    meta = jnp.broadcast_to(meta_tokens[None].astype(x_prompt.dtype), (bp, N_META, D_MODEL))
    hp = jnp.concatenate([meta, x_prompt], axis=1)
    hs = x_sample
    prompt_segments = ((N_META, N_META), (seq, SSD_CHUNK))
    sample_segments = ((ls, math.gcd(ls, SSD_CHUNK)),)
    kp, vp, lfp, ksm, vsm, lfs = [], [], [], [], [], []
    ssm_p, conv_p, ssm_s, conv_s = [], [], [], []
    for i in range(DEPTH):
        j = i // N_MIXERS
        np_ = rmsnorm(hp, norm_mix[i])
        ns_ = rmsnorm(hs, norm_mix[i])
        if i % N_MIXERS == 0:
            op, k1, v1, lf1 = fox_prompt(np_, fox_w_in[j], fox_b_f[j], fox_w_out[j])
            os_, k2, v2, lf2 = fox_sample(ns_, fox_w_in[j], fox_b_f[j], fox_w_out[j],
                                          cache_k[j], cache_v[j], cache_lf[j], page_table)
            kp.append(k1); vp.append(v1); lfp.append(lf1)
            ksm.append(k2); vsm.append(v2); lfs.append(lf2)
        else:
            w = (ssd_w_in[j], ssd_conv_w[j], ssd_conv_b[j], ssd_dt_bias[j], ssd_a_log[j], ssd_d[j], ssd_norm[j], ssd_w_out[j])
            ctx0 = jnp.zeros((bp, SSD_CONV - 1, SSD_CONV_DIM), hp.dtype)
            h00 = jnp.zeros((bp, SSD_HEADS, SSD_HEAD_DIM, SSD_STATE), jnp.float32)
            op, s1, c1 = ssd_mixer(np_, ctx0, h00, prompt_segments, *w)
            os_, s2, c2 = ssd_mixer(ns_, state_conv[j], state_ssm[j], sample_segments, *w)
            ssm_p.append(s1); conv_p.append(c1); ssm_s.append(s2); conv_s.append(c2)
        hp = hp + op
        hs = hs + os_
        hp = hp + swiglu(rmsnorm(hp, norm_ffn[i]), ffn_w_gate[i], ffn_w_up[i], ffn_w_down[i])
        hs = hs + swiglu(rmsnorm(hs, norm_ffn[i]), ffn_w_gate[i], ffn_w_up[i], ffn_w_down[i])
    y_prompt = rmsnorm(hp, norm_final)[:, N_META:]
    y_sample = rmsnorm(hs, norm_final)
    k_prompt = jnp.stack(kp)
    v_prompt = jnp.stack(vp)
    lf_prompt = jnp.stack(lfp)
    ssm_prompt = jnp.stack(ssm_p)
    conv_prompt = jnp.stack(conv_p)
    k_sample = jnp.stack(ksm)
    v_sample = jnp.stack(vsm)
    lf_sample = jnp.stack(lfs)
    ssm_sample = jnp.stack(ssm_s)
    conv_sample = jnp.stack(conv_s)
    return (y_prompt, y_sample, k_prompt, v_prompt, lf_prompt, ssm_prompt, conv_prompt,
            k_sample, v_sample, lf_sample, ssm_sample, conv_sample)
```

```python
import functools
import math

import jax
import jax.numpy as jnp
from jax import lax
from jax.experimental import pallas as pl
from jax.experimental.pallas import tpu as pltpu

F32 = jnp.float32
BF16 = jnp.bfloat16

D_MODEL = 1024
N_META = 16
FOX_HEADS = 16
FOX_HEAD_DIM = 64
PAGE_SIZE = 128
SSD_D_INNER = 2048
SSD_HEAD_DIM = 64
SSD_HEADS = 32
SSD_GROUPS = 4
SSD_HPG = 8
SSD_STATE = 128
SSD_CONV = 4
SSD_CONV_DIM = 3072
SSD_CHUNK = 128
SSD_GROUP_WIDTH = SSD_HPG * SSD_HEAD_DIM
FFN_HIDDEN = 2816
RMS_EPS = 1e-6
NEG_BIG = -1e30

LANES = 128
VMEM_LIMIT = 56 * 1024 * 1024
FLASH_BLOCK = 256
FFN_CHUNK = 256


def _cparams(*sem):
    return pltpu.CompilerParams(dimension_semantics=sem, vmem_limit_bytes=VMEM_LIMIT)


def _dot(a, b):
    return jnp.dot(a, b, preferred_element_type=F32)


def _dot_nt(a, b):
    return lax.dot_general(a, b, (((1,), (1,)), ((), ())), preferred_element_type=F32)


def _dot_tn(a, b):
    return lax.dot_general(a, b, (((0,), (0,)), ((), ())), preferred_element_type=F32)


def _split3(x):
    hi = x.astype(BF16)
    r1 = x - hi.astype(F32)
    mid = r1.astype(BF16)
    lo = (r1 - mid.astype(F32)).astype(BF16)
    return hi, mid, lo


def _dot3_r(a_exact, x):
    hi, mid, lo = _split3(x)
    return _dot(a_exact, lo) + _dot(a_exact, mid) + _dot(a_exact, hi)


def _dot3_l(x, b_exact):
    hi, mid, lo = _split3(x)
    return _dot(lo, b_exact) + _dot(mid, b_exact) + _dot(hi, b_exact)


def _dot3_nt_r(a_exact, x):
    hi, mid, lo = _split3(x)
    return _dot_nt(a_exact, lo) + _dot_nt(a_exact, mid) + _dot_nt(a_exact, hi)


def _rms(x, g):
    ms = jnp.mean(x * x, axis=-1, keepdims=True)
    return x * lax.rsqrt(ms + RMS_EPS) * g


def _log_sigmoid(x):
    return jnp.minimum(x, 0.0) - jnp.log1p(jnp.exp(-jnp.abs(x)))


def _softplus(x):
    return jnp.maximum(x, 0.0) + jnp.log1p(jnp.exp(-jnp.abs(x)))


def _silu(x):
    return x * (1.0 / (1.0 + jnp.exp(-x)))


def _iota2(shape, dim):
    return lax.broadcasted_iota(jnp.int32, shape, dim)


def _tile(m, candidates):
    for c in candidates:
        if m % c == 0:
            return c
    return m


def _const_spec(shape):
    nd = len(shape)
    return pl.BlockSpec(shape, lambda *_: (0,) * nd)


def _fox_in_kernel(x_ref, g_ref, wq_ref, wk_ref, wv_ref, wf_ref, bf_ref,
                   q_ref, k_ref, v_ref, kb_ref, vb_ref, lf_ref):
    xn = _rms(x_ref[...], g_ref[...]).astype(BF16)
    q = _dot(xn, wq_ref[...]) * (FOX_HEAD_DIM ** -0.5)
    q_ref[...] = q.astype(q_ref.dtype)
    k = _dot(xn, wk_ref[...])
    k_ref[...] = k
    kb_ref[...] = k.astype(BF16)
    v = _dot(xn, wv_ref[...])
    v_ref[...] = v
    vb_ref[...] = v.astype(BF16)
    lf_ref[...] = _log_sigmoid(_dot(xn, wf_ref[...]) + bf_ref[...])


def fox_in(x, g, wq, wk, wv, wf, b_f, q_dtype):
    m = x.shape[0]
    tm = _tile(m, (384, 256, 128, 144, 96, 48, 16))
    row = lambda w: pl.BlockSpec((tm, w), lambda i: (i, 0))
    return pl.pallas_call(
        _fox_in_kernel,
        grid=(m // tm,),
        in_specs=[row(D_MODEL), _const_spec((1, D_MODEL)),
                  _const_spec((D_MODEL, D_MODEL)), _const_spec((D_MODEL, D_MODEL)),
                  _const_spec((D_MODEL, D_MODEL)), _const_spec((D_MODEL, FOX_HEADS)),
                  _const_spec((1, FOX_HEADS))],
        out_specs=[row(D_MODEL)] * 5 + [row(FOX_HEADS)],
        out_shape=[jax.ShapeDtypeStruct((m, D_MODEL), q_dtype),
                   jax.ShapeDtypeStruct((m, D_MODEL), F32),
                   jax.ShapeDtypeStruct((m, D_MODEL), F32),
                   jax.ShapeDtypeStruct((m, D_MODEL), BF16),
                   jax.ShapeDtypeStruct((m, D_MODEL), BF16),
                   jax.ShapeDtypeStruct((m, FOX_HEADS), F32)],
        compiler_params=_cparams("parallel"),
        name="fox_in",
    )(x, g, wq, wk, wv, wf, b_f)


def _ssd_in_kernel(x_ref, g_ref, wz_ref, wx_ref, wdt_ref, dtb_ref, z_ref, xbc_ref, dt_ref):
    xn = _rms(x_ref[...], g_ref[...]).astype(BF16)
    z_ref[...] = _dot(xn, wz_ref[...])
    xbc_ref[...] = _dot(xn, wx_ref[...])
    dt_ref[...] = _softplus(_dot(xn, wdt_ref[...]) + dtb_ref[...])


def ssd_in(x, g, wz, wx, wdt, dt_bias):
    m = x.shape[0]
    tm = _tile(m, (384, 256, 128, 144, 96, 48, 16))
    row = lambda w: pl.BlockSpec((tm, w), lambda i: (i, 0))
    return pl.pallas_call(
        _ssd_in_kernel,
        grid=(m // tm,),
        in_specs=[row(D_MODEL), _const_spec((1, D_MODEL)),
                  _const_spec((D_MODEL, SSD_D_INNER)), _const_spec((D_MODEL, SSD_CONV_DIM)),
                  _const_spec((D_MODEL, SSD_HEADS)), _const_spec((1, SSD_HEADS))],
        out_specs=[row(SSD_D_INNER), row(SSD_CONV_DIM), row(SSD_HEADS)],
        out_shape=[jax.ShapeDtypeStruct((m, SSD_D_INNER), F32),
                   jax.ShapeDtypeStruct((m, SSD_CONV_DIM), F32),
                   jax.ShapeDtypeStruct((m, SSD_HEADS), F32)],
        compiler_params=_cparams("parallel"),
        name="ssd_in",
    )(x, g, wz, wx, wdt, dt_bias)


def _matmul_res_kernel(a_ref, w_ref, r_ref, o_ref):
    o_ref[...] = r_ref[...] + _dot(a_ref[...], w_ref[...])


def matmul_res(a, w, res):
    m, k = a.shape
    n = w.shape[1]
    tm = _tile(m, (384, 256, 128, 144, 96, 48, 16))
    return pl.pallas_call(
        _matmul_res_kernel,
        grid=(m // tm,),
        in_specs=[pl.BlockSpec((tm, k), lambda i: (i, 0)), _const_spec((k, n)),
                  pl.BlockSpec((tm, n), lambda i: (i, 0))],
        out_specs=pl.BlockSpec((tm, n), lambda i: (i, 0)),
        out_shape=jax.ShapeDtypeStruct((m, n), F32),
        compiler_params=_cparams("parallel"),
        name="matmul_res",
    )(a, w, res)


def _ffn_kernel(x_ref, g_ref, wg_ref, wu_ref, wd_ref, gf_ref, o_ref, acc_ref, *, final_norm):
    x = x_ref[...]
    xn = _rms(x, g_ref[...]).astype(BF16)
    acc_ref[...] = x
    for c in range(FFN_HIDDEN // FFN_CHUNK):
        cols = slice(c * FFN_CHUNK, (c + 1) * FFN_CHUNK)
        gate = _dot(xn, wg_ref[:, cols])
        up = _dot(xn, wu_ref[:, cols])
        act = (_silu(gate) * up).astype(BF16)
        acc_ref[...] += _dot(act, wd_ref[cols, :])
    out = acc_ref[...]
    o_ref[...] = _rms(out, gf_ref[...]) if final_norm else out


def ffn(x, g, wg, wu, wd, g_final, final_norm):
    m = x.shape[0]
    tm = _tile(m, (688, 512, 256, 128, 144, 96, 48, 16))
    return pl.pallas_call(
        functools.partial(_ffn_kernel, final_norm=final_norm),
        grid=(m // tm,),
        in_specs=[pl.BlockSpec((tm, D_MODEL), lambda i: (i, 0)), _const_spec((1, D_MODEL)),
                  _const_spec((D_MODEL, FFN_HIDDEN)), _const_spec((D_MODEL, FFN_HIDDEN)),
                  _const_spec((FFN_HIDDEN, D_MODEL)), _const_spec((1, D_MODEL))],
        out_specs=pl.BlockSpec((tm, D_MODEL), lambda i: (i, 0)),
        out_shape=jax.ShapeDtypeStruct((m, D_MODEL), F32),
        scratch_shapes=[pltpu.VMEM((tm, D_MODEL), F32)],
        compiler_params=_cparams("parallel"),
        name="ffn",
    )(x, g, wg, wu, wd, g_final)


def _fox_cumsum_kernel(lf_ref, f_ref, ft_ref, carry_ref, *, seq_len):
    i = pl.program_id(1)

    @pl.when(i == 0)
    def _():
        carry_ref[...] = jnp.zeros_like(carry_ref)

    lf = lf_ref[0]
    if seq_len % LANES:
        rows = i * LANES + _iota2(lf.shape, 0)
        lf = jnp.where(rows < seq_len, lf, 0.0)
    tril = (_iota2((LANES, LANES), 0) >= _iota2((LANES, LANES), 1)).astype(BF16)
    f = carry_ref[...] + _dot3_r(tril, lf)
    f_ref[0] = f
    carry_ref[...] = f[LANES - 1:LANES, :]
    eye = (_iota2((FOX_HEADS, FOX_HEADS), 0) == _iota2((FOX_HEADS, FOX_HEADS), 1)).astype(BF16)
    ft_ref[0] = _dot3_nt_r(eye, f)


def fox_cumsum(lf):
    b, t, h = lf.shape
    nblk = pl.cdiv(t, LANES)
    return pl.pallas_call(
        functools.partial(_fox_cumsum_kernel, seq_len=t),
        grid=(b, nblk),
        in_specs=[pl.BlockSpec((1, LANES, h), lambda bi, i: (bi, i, 0))],
        out_specs=[pl.BlockSpec((1, LANES, h), lambda bi, i: (bi, i, 0)),
                   pl.BlockSpec((1, h, LANES), lambda bi, i: (bi, 0, i))],
        out_shape=[jax.ShapeDtypeStruct((b, t, h), F32), jax.ShapeDtypeStruct((b, h, t), F32)],
        scratch_shapes=[pltpu.VMEM((1, h), F32)],
        compiler_params=_cparams("parallel", "arbitrary"),
        name="fox_cumsum",
    )(lf)


def _flash_kernel(q_ref, k_ref, v_ref, fq_ref, ft_ref, o_ref, *, seq_len):
    blk = FLASH_BLOCK
    n_full = seq_len // blk
    tail = seq_len % blk
    qi = pl.program_id(1)

    def attend(rows, n_before, diag_start, diag_len):
        lane = _iota2((rows, LANES), 1)
        lower = lane < FOX_HEAD_DIM
        causal = _iota2((rows, diag_len), 1) <= _iota2((rows, diag_len), 0)
        for pair in range(FOX_HEADS // 2):
            cols = slice(pair * LANES, (pair + 1) * LANES)
            qp = q_ref[0, 0:rows, cols]
            zero = jnp.zeros_like(qp)
            q_heads = (jnp.where(lower, qp, zero), jnp.where(lower, zero, qp))
            fq = fq_ref[0, 0:rows, :]
            fq_heads = (fq[:, 2 * pair:2 * pair + 1], fq[:, 2 * pair + 1:2 * pair + 2])

            def scores(hh, kc, fk, mask):
                s = _dot_nt(q_heads[hh], kc) + (fq_heads[hh] - fk)
                if mask is not None:
                    s = jnp.where(mask, s, NEG_BIG)
                return s

            def update(carry, kc, vc, fks, mask):
                m0, l0, m1, l1, acc = carry
                outs = []
                for hh, (m_old, l_old) in enumerate(((m0, l0), (m1, l1))):
                    s = scores(hh, kc, fks[hh], mask)
                    m_new = jnp.maximum(m_old, jnp.max(s, axis=-1, keepdims=True))
                    alpha = jnp.exp(m_old - m_new)
                    p = jnp.exp(s - m_new)
                    l_new = alpha * l_old + jnp.sum(p, axis=-1, keepdims=True)
                    pv = _dot(p.astype(BF16), vc)
                    outs.append((m_new, l_new, alpha, pv))
                (m0, l0, a0, pv0), (m1, l1, a1, pv1) = outs
                acc = acc * jnp.where(lower, a0, a1) + jnp.where(lower, pv0, pv1)
                return m0, l0, m1, l1, acc

            def body(c, carry):
                start = pl.multiple_of(c * blk, blk)
                kc = k_ref[0, pl.ds(start, blk), cols]
                vc = v_ref[0, pl.ds(start, blk), cols]
                fks = (ft_ref[0, 2 * pair:2 * pair + 1, pl.ds(start, blk)],
                       ft_ref[0, 2 * pair + 1:2 * pair + 2, pl.ds(start, blk)])
                return update(carry, kc, vc, fks, None)

            init = (jnp.full((rows, 1), NEG_BIG, F32), jnp.zeros((rows, 1), F32),
                    jnp.full((rows, 1), NEG_BIG, F32), jnp.zeros((rows, 1), F32),
                    jnp.zeros((rows, LANES), F32))
            carry = lax.fori_loop(0, n_before, body, init)
            kc = k_ref[0, pl.ds(diag_start, diag_len), cols]
            vc = v_ref[0, pl.ds(diag_start, diag_len), cols]
            fks = (ft_ref[0, 2 * pair:2 * pair + 1, pl.ds(diag_start, diag_len)],
                   ft_ref[0, 2 * pair + 1:2 * pair + 2, pl.ds(diag_start, diag_len)])
            m0, l0, m1, l1, acc = update(carry, kc, vc, fks, causal)
            out = acc * (1.0 / jnp.where(lower, l0, l1))
            o_ref[0, 0:rows, cols] = out.astype(o_ref.dtype)

    if n_full:
        @pl.when(qi < n_full)
        def _():
            attend(blk, qi, pl.multiple_of(qi * blk, blk), blk)

    if tail:
        @pl.when(qi == n_full)
        def _():
            attend(tail, n_full, n_full * blk, tail)


def fox_flash(q, k, v, f, ft):
    b, t, d = q.shape
    nq = pl.cdiv(t, FLASH_BLOCK)
    return pl.pallas_call(
        functools.partial(_flash_kernel, seq_len=t),
        grid=(b, nq),
        in_specs=[pl.BlockSpec((1, FLASH_BLOCK, d), lambda bi, i: (bi, i, 0)),
                  pl.BlockSpec((1, t, d), lambda bi, i: (bi, 0, 0)),
                  pl.BlockSpec((1, t, d), lambda bi, i: (bi, 0, 0)),
                  pl.BlockSpec((1, FLASH_BLOCK, FOX_HEADS), lambda bi, i: (bi, i, 0)),
                  pl.BlockSpec((1, FOX_HEADS, t), lambda bi, i: (bi, 0, 0))],
        out_specs=pl.BlockSpec((1, FLASH_BLOCK, d), lambda bi, i: (bi, i, 0)),
        out_shape=jax.ShapeDtypeStruct((b, t, d), BF16),
        compiler_params=_cparams("parallel", "arbitrary"),
        name="fox_flash",
    )(q, k, v, f, ft)


def _sample_f_kernel(pt_ref, clf_ref, lfn_ref, fkt_ref, fqt_ref, carry_ref, *, n_pages):
    p = pl.program_id(1)

    @pl.when(p == 0)
    def _():
        carry_ref[...] = jnp.zeros_like(carry_ref)

    lf = clf_ref[0, 0]
    tril = (_iota2((PAGE_SIZE, PAGE_SIZE), 0) >= _iota2((PAGE_SIZE, PAGE_SIZE), 1)).astype(BF16)
    f = carry_ref[...] + _dot3_r(tril, lf)
    carry_ref[...] = f[PAGE_SIZE - 1:PAGE_SIZE, :]
    eye = (_iota2((FOX_HEADS, FOX_HEADS), 0) == _iota2((FOX_HEADS, FOX_HEADS), 1)).astype(BF16)
    fkt_ref[0] = _dot3_nt_r(eye, f)

    @pl.when(p == n_pages - 1)
    def _():
        lfn = lfn_ref[0]
        n_new = lfn.shape[0]
        tril_n = (_iota2((n_new, n_new), 0) >= _iota2((n_new, n_new), 1)).astype(BF16)
        fq = carry_ref[...] + _dot3_r(tril_n, lfn)
        fqt_ref[0] = _dot3_nt_r(eye, fq)


def sample_f(page_table_flat, cache_lf, layer, lf_new, n_pages):
    b, n_new, h = lf_new.shape
    grid_spec = pltpu.PrefetchScalarGridSpec(
        num_scalar_prefetch=1,
        grid=(b, n_pages),
        in_specs=[pl.BlockSpec((1, 1, PAGE_SIZE, h), lambda bi, p, pt: (layer, pt[bi * n_pages + p], 0, 0)),
                  pl.BlockSpec((1, n_new, h), lambda bi, p, pt: (bi, 0, 0))],
        out_specs=[pl.BlockSpec((1, h, PAGE_SIZE), lambda bi, p, pt: (bi, 0, p)),
                   pl.BlockSpec((1, h, n_new), lambda bi, p, pt: (bi, 0, 0))],
        scratch_shapes=[pltpu.VMEM((1, h), F32)],
    )
    return pl.pallas_call(
        functools.partial(_sample_f_kernel, n_pages=n_pages),
        grid_spec=grid_spec,
        out_shape=[jax.ShapeDtypeStruct((b, h, n_pages * PAGE_SIZE), F32),
                   jax.ShapeDtypeStruct((b, h, n_new), F32)],
        compiler_params=_cparams("parallel", "arbitrary"),
        name="sample_f",
    )(page_table_flat, cache_lf, lf_new)


def _sample_attn_kernel(pt_ref, q_ref, ck_ref, cv_ref, fkt_ref, fqt_ref, kn_ref, vn_ref, o_ref,
                        qbd_ref, fqb_ref, m_ref, l_ref, acc_ref, *, n_pages, n_new):
    p = pl.program_id(1)
    rows = n_new * FOX_HEADS
    head_of_row = _iota2((rows, D_MODEL), 0) % FOX_HEADS
    head_of_col = _iota2((rows, D_MODEL), 1) // FOX_HEAD_DIM
    diag = head_of_row == head_of_col

    @pl.when(p == 0)
    def _():
        q = q_ref[0]
        qrep = jnp.concatenate(
            [jnp.broadcast_to(q[i:i + 1, :], (FOX_HEADS, D_MODEL)) for i in range(n_new)], axis=0)
        qbd_ref[...] = jnp.where(diag, qrep, 0.0).astype(BF16)
        fqt = fqt_ref[0]
        fqb_ref[...] = jnp.concatenate(
            [jnp.broadcast_to(fqt[:, i:i + 1], (FOX_HEADS, LANES)) for i in range(n_new)], axis=0)
        m_ref[...] = jnp.full_like(m_ref, NEG_BIG)
        l_ref[...] = jnp.zeros_like(l_ref)
        acc_ref[...] = jnp.zeros_like(acc_ref)

    def update(s, vb):
        m_old = m_ref[...]
        m_new = jnp.maximum(m_old, jnp.max(s, axis=-1, keepdims=True))
        alpha = jnp.exp(m_old - m_new)
        pr = jnp.exp(s - m_new)
        l_ref[...] = alpha * l_ref[...] + jnp.sum(pr, axis=-1, keepdims=True)
        m_ref[...] = m_new
        acc_ref[...] = acc_ref[...] * alpha + _dot(pr.astype(BF16), vb)

    fk = jnp.concatenate([fkt_ref[0]] * n_new, axis=0)
    s = _dot_nt(qbd_ref[...], ck_ref[0, 0].astype(BF16)) + (fqb_ref[...] - fk)
    update(s, cv_ref[0, 0].astype(BF16))

    @pl.when(p == n_pages - 1)
    def _():
        fqt = fqt_ref[0]
        fk_new = jnp.concatenate([fqt] * n_new, axis=0)
        s_new = _dot_nt(qbd_ref[...], kn_ref[0].astype(BF16)) + (fqb_ref[:, 0:n_new] - fk_new)
        causal = _iota2((rows, n_new), 1) <= _iota2((rows, n_new), 0) // FOX_HEADS
        update(jnp.where(causal, s_new, NEG_BIG), vn_ref[0].astype(BF16))
        out = acc_ref[...] * (1.0 / l_ref[...])
        out = jnp.where(diag, out, 0.0).astype(BF16)
        pick = (_iota2((n_new, rows), 1) // FOX_HEADS == _iota2((n_new, rows), 0)).astype(BF16)
        o_ref[0] = _dot(pick, out).astype(o_ref.dtype)


def sample_attn(page_table_flat, q, cache_k, cache_v, layer, fkt, fqt, k_new, v_new, n_pages):
    b, n_new, d = q.shape
    rows = n_new * FOX_HEADS
    seq = lambda shape: pl.BlockSpec((1,) + shape, lambda bi, p, pt: (bi, 0, 0))
    page = pl.BlockSpec((1, 1, PAGE_SIZE, d), lambda bi, p, pt: (layer, pt[bi * n_pages + p], 0, 0))
    grid_spec = pltpu.PrefetchScalarGridSpec(
        num_scalar_prefetch=1,
        grid=(b, n_pages),
        in_specs=[seq((n_new, d)), page, page,
                  pl.BlockSpec((1, FOX_HEADS, PAGE_SIZE), lambda bi, p, pt: (bi, 0, p)),
                  seq((FOX_HEADS, n_new)), seq((n_new, d)), seq((n_new, d))],
        out_specs=seq((n_new, d)),
        scratch_shapes=[pltpu.VMEM((rows, d), BF16), pltpu.VMEM((rows, LANES), F32),
                        pltpu.VMEM((rows, 1), F32), pltpu.VMEM((rows, 1), F32),
                        pltpu.VMEM((rows, d), F32)],
    )
    return pl.pallas_call(
        functools.partial(_sample_attn_kernel, n_pages=n_pages, n_new=n_new),
        grid_spec=grid_spec,
        out_shape=jax.ShapeDtypeStruct((b, n_new, d), BF16),
        compiler_params=_cparams("parallel", "arbitrary"),
        name="sample_attn",
    )(page_table_flat, q, cache_k, cache_v, fkt, fqt, k_new, v_new)


CONV_PAD = 8


def _ssd_scan_kernel(xbc_ref, z_ref, dt_ref, h0_ref, ctx_ref, cw_ref, cb_ref, alog_ref, dexp_ref,
                     nw_ref, e_ref, et_ref, g_ref, hout_ref, cout_ref, h_ref, ext_ref,
                     *, seq_len, chunk):
    c = pl.program_id(1)
    n_chunks = pl.cdiv(seq_len, chunk)
    last_valid = seq_len - (n_chunks - 1) * chunk
    tail0 = CONV_PAD - (SSD_CONV - 1)

    @pl.when(c == 0)
    def _():
        h_ref[...] = h0_ref[0]
        ext_ref[tail0:CONV_PAD, :] = ctx_ref[0]

    ext_ref[CONV_PAD:CONV_PAD + chunk, :] = xbc_ref[0]
    cw = cw_ref[...]
    xc = cb_ref[...]
    for w in range(SSD_CONV):
        xc = xc + cw[w:w + 1, :] * ext_ref[tail0 + w:tail0 + w + chunk, :]
    xc = _silu(xc)
    dt = dt_ref[0]
    if last_valid != chunk:
        n_valid = jnp.where(c == n_chunks - 1, last_valid, chunk)
        xc = jnp.where(_iota2(xc.shape, 0) < n_valid, xc, 0.0)
        dt = jnp.where(_iota2(dt.shape, 0) < n_valid, dt, 0.0)
    xs = xc[:, :SSD_D_INNER]
    b_all = xc[:, SSD_D_INNER:SSD_D_INNER + SSD_GROUPS * SSD_STATE].astype(BF16)
    c_all = xc[:, SSD_D_INNER + SSD_GROUPS * SSD_STATE:].astype(BF16)

    a_neg = -jnp.exp(alog_ref[...])
    tril = (_iota2((chunk, chunk), 0) >= _iota2((chunk, chunk), 1))
    a_cum = _dot3_r(tril.astype(BF16), dt * a_neg)
    eye = (_iota2((SSD_HEADS, SSD_HEADS), 0) == _iota2((SSD_HEADS, SSD_HEADS), 1)).astype(BF16)
    a_cum_t = _dot3_nt_r(eye, a_cum)
    expand = e_ref[...]
    dt_e = _dot3_l(dt, expand)
    ac_e = _dot3_l(a_cum, expand)
    al_e = ac_e[chunk - 1:chunk, :]
    xdt = xs * dt_e
    xdt_b = xdt.astype(BF16)
    x_state = (xdt * jnp.exp(al_e - ac_e)).astype(BF16)
    ea_e = jnp.exp(ac_e)
    a_last_rows = jnp.broadcast_to(a_cum_t[:, chunk - 1:chunk], (SSD_HEADS, SSD_STATE))
    decay_rows = jnp.exp(_dot3_r(et_ref[...], a_last_rows))

    lower = _iota2((chunk, LANES), 1) < SSD_HEAD_DIM
    y_groups = []
    for g in range(SSD_GROUPS):
        gcols = slice(g * SSD_GROUP_WIDTH, (g + 1) * SSD_GROUP_WIDTH)
        cg = c_all[:, g * SSD_STATE:(g + 1) * SSD_STATE]
        bg = b_all[:, g * SSD_STATE:(g + 1) * SSD_STATE]
        cb = _dot_nt(cg, bg)
        h_old = h_ref[gcols, :]
        y_off = _dot_nt(cg, h_old.astype(BF16)) * ea_e[:, gcols]
        pairs = []
        for pr in range(SSD_HPG // 2):
            pcols = slice(g * SSD_GROUP_WIDTH + pr * LANES, g * SSD_GROUP_WIDTH + (pr + 1) * LANES)
            halves = []
            for hh in range(2):
                head = g * SSD_HPG + 2 * pr + hh
                diff = a_cum[:, head:head + 1] - a_cum_t[head:head + 1, :]
                lmat = jnp.where(tril, jnp.exp(diff), 0.0)
                halves.append(_dot((cb * lmat).astype(BF16), xdt_b[:, pcols]))
            pairs.append(jnp.where(lower, halves[0], halves[1]))
        y_groups.append(jnp.concatenate(pairs, axis=1) + y_off)
        h_ref[gcols, :] = decay_rows[gcols, :] * h_old + _dot_tn(x_state[:, gcols], bg)

    y = jnp.concatenate(y_groups, axis=1) + dexp_ref[...] * xs
    gated = y * _silu(z_ref[0])
    normed = []
    for g in range(SSD_GROUPS):
        gg = gated[:, g * SSD_GROUP_WIDTH:(g + 1) * SSD_GROUP_WIDTH]
        normed.append(gg * lax.rsqrt(jnp.mean(gg * gg, axis=-1, keepdims=True) + RMS_EPS))
    g_ref[0] = (jnp.concatenate(normed, axis=1) * nw_ref[...]).astype(g_ref.dtype)

    if n_chunks > 1:
        @pl.when(c < n_chunks - 1)
        def _():
            ext_ref[tail0:CONV_PAD, :] = ext_ref[tail0 + chunk:CONV_PAD + chunk, :]

    @pl.when(c == n_chunks - 1)
    def _():
        hout_ref[0] = h_ref[...]
        cout_ref[0] = ext_ref[tail0 + last_valid:CONV_PAD + last_valid, :]


def ssd_scan(xbc, z, dt, h0, ctx, conv_w, conv_b, a_log, d_exp, norm_w, expand, expand_t, chunk):
    b, t, _ = xbc.shape
    n_chunks = pl.cdiv(t, chunk)
    tok = lambda w: pl.BlockSpec((1, chunk, w), lambda bi, c: (bi, c, 0))
    per_seq = lambda shape: pl.BlockSpec((1,) + shape, lambda bi, c: (bi, 0, 0))
    return pl.pallas_call(
        functools.partial(_ssd_scan_kernel, seq_len=t, chunk=chunk),
        grid=(b, n_chunks),
        in_specs=[tok(SSD_CONV_DIM), tok(SSD_D_INNER), tok(SSD_HEADS),
                  per_seq((SSD_D_INNER, SSD_STATE)), per_seq((SSD_CONV - 1, SSD_CONV_DIM)),
                  _const_spec((SSD_CONV, SSD_CONV_DIM)), _const_spec((1, SSD_CONV_DIM)),
                  _const_spec((1, SSD_HEADS)), _const_spec((1, SSD_D_INNER)),
                  _const_spec((1, SSD_D_INNER)), _const_spec((SSD_HEADS, SSD_D_INNER)),
                  _const_spec((SSD_D_INNER, SSD_HEADS))],
        out_specs=[tok(SSD_D_INNER), per_seq((SSD_D_INNER, SSD_STATE)),
                   per_seq((SSD_CONV - 1, SSD_CONV_DIM))],
        out_shape=[jax.ShapeDtypeStruct((b, t, SSD_D_INNER), BF16),
                   jax.ShapeDtypeStruct((b, SSD_D_INNER, SSD_STATE), F32),
                   jax.ShapeDtypeStruct((b, SSD_CONV - 1, SSD_CONV_DIM), F32)],
        scratch_shapes=[pltpu.VMEM((SSD_D_INNER, SSD_STATE), F32),
                        pltpu.VMEM((CONV_PAD + chunk, SSD_CONV_DIM), F32)],
        compiler_params=_cparams("parallel", "arbitrary"),
        name="ssd_scan",
    )(xbc, z, dt, h0, ctx, conv_w, conv_b, a_log, d_exp, norm_w, expand, expand_t)


def kernel(x_prompt, x_sample, cache_k, cache_v, cache_lf, state_ssm, state_conv, page_table,
           meta_tokens, norm_mix, norm_ffn, norm_final, fox_w_in, fox_b_f, fox_w_out,
           ssd_w_in, ssd_conv_w, ssd_conv_b, ssd_dt_bias, ssd_a_log, ssd_d, ssd_norm, ssd_w_out,
           ffn_w_gate, ffn_w_up, ffn_w_down):
    bp, seq, d = x_prompt.shape
    bs, ls, _ = x_sample.shape
    depth = norm_mix.shape[0]
    t = seq + N_META
    n_pages = page_table.shape[1]
    n_phys = cache_k.shape[1]
    pt_flat = page_table.reshape(-1).astype(jnp.int32)
    ck = cache_k.reshape(cache_k.shape[0], n_phys, PAGE_SIZE, d)
    cv = cache_v.reshape(cache_v.shape[0], n_phys, PAGE_SIZE, d)

    meta = jnp.broadcast_to(meta_tokens[None].astype(x_prompt.dtype), (bp, N_META, d))
    hp = jnp.concatenate([meta, x_prompt], axis=1).reshape(bp * t, d)
    hs = x_sample.reshape(bs * ls, d)

    heads = jnp.arange(SSD_D_INNER, dtype=jnp.int32) // SSD_HEAD_DIM
    expand = (heads[None, :] == jnp.arange(SSD_HEADS, dtype=jnp.int32)[:, None]).astype(BF16)
    expand_t = expand.T
    row = lambda v: v.reshape(1, -1)
    sample_chunk = math.gcd(ls, SSD_CHUNK)

    outs = {n: [] for n in ("kp", "vp", "lfp", "ks", "vs", "lfs", "ssm_p", "conv_p", "ssm_s", "conv_s")}
    for i in range(depth):
        j = i // 2
        g_mix = row(norm_mix[i])
        if i % 2 == 0:
            w_in = fox_w_in[j].astype(BF16)
            wq, wk, wv = w_in[:, :d], w_in[:, d:2 * d], w_in[:, 2 * d:3 * d]
            wf = w_in[:, 3 * d:]
            b_f = row(fox_b_f[j])
            w_out = fox_w_out[j].astype(BF16)
            q, k, v, kb, vb, lf = fox_in(hp, g_mix, wq, wk, wv, wf, b_f, BF16)
            f, ft = fox_cumsum(lf.reshape(bp, t, FOX_HEADS))
            o = fox_flash(q.reshape(bp, t, d), kb.reshape(bp, t, d), vb.reshape(bp, t, d), f, ft)
            hp = matmul_res(o.reshape(bp * t, d), w_out, hp)
            outs["kp"].append(k.reshape(bp, t, FOX_HEADS, FOX_HEAD_DIM))
            outs["vp"].append(v.reshape(bp, t, FOX_HEADS, FOX_HEAD_DIM))
            outs["lfp"].append(lf.reshape(bp, t, FOX_HEADS))
            q, k, v, _, _, lf = fox_in(hs, g_mix, wq, wk, wv, wf, b_f, F32)
            lf3 = lf.reshape(bs, ls, FOX_HEADS)
            fkt, fqt = sample_f(pt_flat, cache_lf, j, lf3, n_pages)
            o = sample_attn(pt_flat, q.reshape(bs, ls, d), ck, cv, j, fkt, fqt,
                            k.reshape(bs, ls, d), v.reshape(bs, ls, d), n_pages)
            hs = matmul_res(o.reshape(bs * ls, d), w_out, hs)
            outs["ks"].append(k.reshape(bs, ls, FOX_HEADS, FOX_HEAD_DIM))
            outs["vs"].append(v.reshape(bs, ls, FOX_HEADS, FOX_HEAD_DIM))
            outs["lfs"].append(lf3)
        else:
            w_in = ssd_w_in[j].astype(BF16)
            wz = w_in[:, :SSD_D_INNER]
            wx = w_in[:, SSD_D_INNER:SSD_D_INNER + SSD_CONV_DIM]
            wdt = w_in[:, SSD_D_INNER + SSD_CONV_DIM:]
            w_out = ssd_w_out[j].astype(BF16)
            d_exp = row(jnp.repeat(ssd_d[j].astype(F32), SSD_HEAD_DIM))
            shared = (ssd_conv_w[j], row(ssd_conv_b[j]), row(ssd_a_log[j]), d_exp, row(ssd_norm[j]),
                      expand, expand_t)
            z, xbc, dt = ssd_in(hp, g_mix, wz, wx, wdt, row(ssd_dt_bias[j]))
            gated, ssm, conv = ssd_scan(
                xbc.reshape(bp, t, SSD_CONV_DIM), z.reshape(bp, t, SSD_D_INNER),
                dt.reshape(bp, t, SSD_HEADS),
                jnp.zeros((bp, SSD_D_INNER, SSD_STATE), F32),
                jnp.zeros((bp, SSD_CONV - 1, SSD_CONV_DIM), F32), *shared, SSD_CHUNK)
            hp = matmul_res(gated.reshape(bp * t, SSD_D_INNER), w_out, hp)
            outs["ssm_p"].append(ssm.reshape(bp, SSD_HEADS, SSD_HEAD_DIM, SSD_STATE))
            outs["conv_p"].append(conv)
            z, xbc, dt = ssd_in(hs, g_mix, wz, wx, wdt, row(ssd_dt_bias[j]))
            gated, ssm, conv = ssd_scan(
                xbc.reshape(bs, ls, SSD_CONV_DIM), z.reshape(bs, ls, SSD_D_INNER),
                dt.reshape(bs, ls, SSD_HEADS),
                state_ssm[j].reshape(bs, SSD_D_INNER, SSD_STATE), state_conv[j], *shared, sample_chunk)
            hs = matmul_res(gated.reshape(bs * ls, SSD_D_INNER), w_out, hs)
            outs["ssm_s"].append(ssm.reshape(bs, SSD_HEADS, SSD_HEAD_DIM, SSD_STATE))
            outs["conv_s"].append(conv)
        last = i == depth - 1
        ffn_w = (row(norm_ffn[i]), ffn_w_gate[i].astype(BF16), ffn_w_up[i].astype(BF16),
                 ffn_w_down[i].astype(BF16), row(norm_final))
        hp = ffn(hp, *ffn_w, last)
        hs = ffn(hs, *ffn_w, last)

    y_prompt = hp.reshape(bp, t, d)[:, N_META:]
    y_sample = hs.reshape(bs, ls, d)
    st = lambda n: jnp.stack(outs[n])
    return (y_prompt, y_sample, st("kp"), st("vp"), st("lfp"), st("ssm_p"), st("conv_p"),
            st("ks"), st("vs"), st("lfs"), st("ssm_s"), st("conv_s"))
```

```python
import functools
import math

import jax
import jax.numpy as jnp
from jax import lax
from jax.experimental import pallas as pl
from jax.experimental.pallas import tpu as pltpu

F32 = jnp.float32
BF16 = jnp.bfloat16

D_MODEL = 1024
N_META = 16
FOX_HEADS = 16
FOX_HEAD_DIM = 64
PAGE_SIZE = 128
SSD_D_INNER = 2048
SSD_HEAD_DIM = 64
SSD_HEADS = 32
SSD_GROUPS = 4
SSD_HPG = 8
SSD_STATE = 128
SSD_CONV = 4
SSD_CONV_DIM = 3072
SSD_CHUNK = 128
SSD_GROUP_WIDTH = SSD_HPG * SSD_HEAD_DIM
FFN_HIDDEN = 2816
RMS_EPS = 1e-6
NEG_BIG = -1e30

LANES = 128
VMEM_LIMIT = 56 * 1024 * 1024
FLASH_BLOCK = 256
FOX_IN_TILE = 384
FFN_CHUNK = 256


def _cparams(*sem):
    return pltpu.CompilerParams(dimension_semantics=sem, vmem_limit_bytes=VMEM_LIMIT)


def _dot(a, b):
    return jnp.dot(a, b, preferred_element_type=F32)


def _dot_nt(a, b):
    return lax.dot_general(a, b, (((1,), (1,)), ((), ())), preferred_element_type=F32)


def _dot_tn(a, b):
    return lax.dot_general(a, b, (((0,), (0,)), ((), ())), preferred_element_type=F32)


def _split3(x):
    hi = x.astype(BF16)
    r1 = x - hi.astype(F32)
    mid = r1.astype(BF16)
    lo = (r1 - mid.astype(F32)).astype(BF16)
    return hi, mid, lo


def _dot3_r(a_exact, x):
    hi, mid, lo = _split3(x)
    return _dot(a_exact, lo) + _dot(a_exact, mid) + _dot(a_exact, hi)


def _dot3_l(x, b_exact):
    hi, mid, lo = _split3(x)
    return _dot(lo, b_exact) + _dot(mid, b_exact) + _dot(hi, b_exact)


def _dot3_nt_r(a_exact, x):
    hi, mid, lo = _split3(x)
    return _dot_nt(a_exact, lo) + _dot_nt(a_exact, mid) + _dot_nt(a_exact, hi)


def _rms(x, g):
    ms = jnp.mean(x * x, axis=-1, keepdims=True)
    return x * lax.rsqrt(ms + RMS_EPS) * g


def _log_sigmoid(x):
    return jnp.minimum(x, 0.0) - jnp.log1p(jnp.exp(-jnp.abs(x)))


def _softplus(x):
    return jnp.maximum(x, 0.0) + jnp.log1p(jnp.exp(-jnp.abs(x)))


def _silu(x):
    return x * (1.0 / (1.0 + jnp.exp(-x)))


def _iota2(shape, dim):
    return lax.broadcasted_iota(jnp.int32, shape, dim)


def _tile(m, candidates):
    for c in candidates:
        if m % c == 0:
            return c
    return m


def _const_spec(shape):
    nd = len(shape)
    return pl.BlockSpec(shape, lambda *_: (0,) * nd)


def _ssd_in_kernel(x_ref, g_ref, wz_ref, wx_ref, wdt_ref, dtb_ref, z_ref, xbc_ref, dt_ref):
    xn = _rms(x_ref[...], g_ref[...]).astype(BF16)
    z_ref[...] = _dot(xn, wz_ref[...])
    xbc_ref[...] = _dot(xn, wx_ref[...])
    dt_ref[...] = _softplus(_dot(xn, wdt_ref[...]) + dtb_ref[...])


def ssd_in(x, g, wz, wx, wdt, dt_bias):
    m = x.shape[0]
    tm = _tile(m, (384, 256, 128, 144, 96, 48, 16))
    row = lambda w: pl.BlockSpec((tm, w), lambda i: (i, 0))
    return pl.pallas_call(
        _ssd_in_kernel,
        grid=(m // tm,),
        in_specs=[row(D_MODEL), _const_spec((1, D_MODEL)),
                  _const_spec((D_MODEL, SSD_D_INNER)), _const_spec((D_MODEL, SSD_CONV_DIM)),
                  _const_spec((D_MODEL, SSD_HEADS)), _const_spec((1, SSD_HEADS))],
        out_specs=[row(SSD_D_INNER), row(SSD_CONV_DIM), row(SSD_HEADS)],
        out_shape=[jax.ShapeDtypeStruct((m, SSD_D_INNER), F32),
                   jax.ShapeDtypeStruct((m, SSD_CONV_DIM), F32),
                   jax.ShapeDtypeStruct((m, SSD_HEADS), F32)],
        compiler_params=_cparams("parallel"),
        name="ssd_in",
    )(x, g, wz, wx, wdt, dt_bias)


def _matmul_res_kernel(a_ref, w_ref, r_ref, o_ref):
    o_ref[...] = r_ref[...] + _dot(a_ref[...], w_ref[...])


def matmul_res(a, w, res):
    m, k = a.shape
    n = w.shape[1]
    tm = _tile(m, (384, 256, 128, 144, 96, 48, 16))
    return pl.pallas_call(
        _matmul_res_kernel,
        grid=(m // tm,),
        in_specs=[pl.BlockSpec((tm, k), lambda i: (i, 0)), _const_spec((k, n)),
                  pl.BlockSpec((tm, n), lambda i: (i, 0))],
        out_specs=pl.BlockSpec((tm, n), lambda i: (i, 0)),
        out_shape=jax.ShapeDtypeStruct((m, n), F32),
        compiler_params=_cparams("parallel"),
        name="matmul_res",
    )(a, w, res)


def _ffn_kernel(x_ref, g_ref, wg_ref, wu_ref, wd_ref, gf_ref, o_ref, acc_ref, *, final_norm):
    x = x_ref[...]
    xn = _rms(x, g_ref[...]).astype(BF16)
    acc_ref[...] = x
    for c in range(FFN_HIDDEN // FFN_CHUNK):
        cols = slice(c * FFN_CHUNK, (c + 1) * FFN_CHUNK)
        gate = _dot(xn, wg_ref[:, cols])
        up = _dot(xn, wu_ref[:, cols])
        act = (_silu(gate) * up).astype(BF16)
        acc_ref[...] += _dot(act, wd_ref[cols, :])
    out = acc_ref[...]
    o_ref[...] = _rms(out, gf_ref[...]) if final_norm else out


def ffn(x, g, wg, wu, wd, g_final, final_norm):
    m = x.shape[0]
    tm = _tile(m, (688, 512, 256, 128, 144, 96, 48, 16))
    return pl.pallas_call(
        functools.partial(_ffn_kernel, final_norm=final_norm),
        grid=(m // tm,),
        in_specs=[pl.BlockSpec((tm, D_MODEL), lambda i: (i, 0)), _const_spec((1, D_MODEL)),
                  _const_spec((D_MODEL, FFN_HIDDEN)), _const_spec((D_MODEL, FFN_HIDDEN)),
                  _const_spec((FFN_HIDDEN, D_MODEL)), _const_spec((1, D_MODEL))],
        out_specs=pl.BlockSpec((tm, D_MODEL), lambda i: (i, 0)),
        out_shape=jax.ShapeDtypeStruct((m, D_MODEL), F32),
        scratch_shapes=[pltpu.VMEM((tm, D_MODEL), F32)],
        compiler_params=_cparams("parallel"),
        name="ffn",
    )(x, g, wg, wu, wd, g_final)


def _fox_in_prompt_kernel(x_ref, g_ref, wq_ref, wkt_ref, wvt_ref, wft_ref, bft_ref,
                          q_ref, kt_ref, vt_ref, ktb_ref, vtb_ref, lft_ref):
    xn = _rms(x_ref[0], g_ref[...]).astype(BF16)
    q_ref[0] = (_dot(xn, wq_ref[...]) * (FOX_HEAD_DIM ** -0.5)).astype(BF16)
    kt = _dot_nt(wkt_ref[...], xn)
    kt_ref[0] = kt
    ktb_ref[0] = kt.astype(BF16)
    vt = _dot_nt(wvt_ref[...], xn)
    vt_ref[0] = vt
    vtb_ref[0] = vt.astype(BF16)
    lft_ref[0] = _log_sigmoid(_dot_nt(wft_ref[...], xn) + bft_ref[...])


def fox_in_prompt(x, g, wq, wkt, wvt, wft, b_ft):
    b, t, d = x.shape
    tt = FOX_IN_TILE if t > FOX_IN_TILE else t
    rows = pl.BlockSpec((1, tt, d), lambda bi, i: (bi, i, 0))
    cols = lambda r: pl.BlockSpec((1, r, tt), lambda bi, i: (bi, 0, i))
    return pl.pallas_call(
        _fox_in_prompt_kernel,
        grid=(b, pl.cdiv(t, tt)),
        in_specs=[rows, _const_spec((1, d)), _const_spec((d, d)), _const_spec((d, d)),
                  _const_spec((d, d)), _const_spec((FOX_HEADS, d)), _const_spec((FOX_HEADS, 1))],
        out_specs=[rows, cols(d), cols(d), cols(d), cols(d), cols(FOX_HEADS)],
        out_shape=[jax.ShapeDtypeStruct((b, t, d), BF16),
                   jax.ShapeDtypeStruct((b, d, t), F32), jax.ShapeDtypeStruct((b, d, t), F32),
                   jax.ShapeDtypeStruct((b, d, t), BF16), jax.ShapeDtypeStruct((b, d, t), BF16),
                   jax.ShapeDtypeStruct((b, FOX_HEADS, t), F32)],
        compiler_params=_cparams("parallel", "parallel"),
        name="fox_in_prompt",
    )(x, g, wq, wkt, wvt, wft, b_ft)


def _fox_in_sample_kernel(x_ref, g_ref, wq_ref, wk_ref, wv_ref, wf_ref, bf_ref,
                          q_ref, k_ref, v_ref, lf_ref):
    xn = _rms(x_ref[...], g_ref[...]).astype(BF16)
    q_ref[...] = _dot(xn, wq_ref[...]) * (FOX_HEAD_DIM ** -0.5)
    k_ref[...] = _dot(xn, wk_ref[...])
    v_ref[...] = _dot(xn, wv_ref[...])
    lf_ref[...] = _log_sigmoid(_dot(xn, wf_ref[...]) + bf_ref[...])


def fox_in_sample(x, g, wq, wk, wv, wf, b_f):
    m, d = x.shape
    tm = _tile(m, (256, 128, 64, 16))
    row = lambda w: pl.BlockSpec((tm, w), lambda i: (i, 0))
    return pl.pallas_call(
        _fox_in_sample_kernel,
        grid=(m // tm,),
        in_specs=[row(d), _const_spec((1, d)), _const_spec((d, d)), _const_spec((d, d)),
                  _const_spec((d, d)), _const_spec((d, FOX_HEADS)), _const_spec((1, FOX_HEADS))],
        out_specs=[row(d), row(d), row(d), row(FOX_HEADS)],
        out_shape=[jax.ShapeDtypeStruct((m, d), F32)] * 3 + [jax.ShapeDtypeStruct((m, FOX_HEADS), F32)],
        compiler_params=_cparams("parallel"),
        name="fox_in_sample",
    )(x, g, wq, wk, wv, wf, b_f)


def _flash_kernel(q_ref, kt_ref, vt_ref, lft_ref, o_ref,
                  ft_ref, f_ref, qm_ref, fqb_ref, m_ref, l_ref, acc_ref, *, seq_len):
    blk = FLASH_BLOCK
    n_full = seq_len // blk
    tail = seq_len % blk
    qi = pl.program_id(1)

    @pl.when(qi == 0)
    def _():
        carry = jnp.zeros((FOX_HEADS, 1), F32)
        for start in range(0, seq_len, LANES):
            w = min(LANES, seq_len - start)
            triu = (_iota2((w, w), 0) <= _iota2((w, w), 1)).astype(BF16)
            eye = (_iota2((w, w), 0) == _iota2((w, w), 1)).astype(BF16)
            ft = carry + _dot3_l(lft_ref[0, :, start:start + w], triu)
            ft_ref[:, start:start + w] = ft
            f_ref[start:start + w, :] = _dot3_nt_r(eye, ft)
            carry = ft[:, w - 1:w]

    def attend(rows, row0, n_before, diag_start, diag_len):
        lower = _iota2((rows, LANES), 1) < FOX_HEAD_DIM
        fq = f_ref[pl.ds(row0, rows), :]
        for pair in range(FOX_HEADS // 2):
            qp = q_ref[0, 0:rows, pair * LANES:(pair + 1) * LANES]
            zero = jnp.zeros_like(qp)
            qm_ref[2 * pair, 0:rows, :] = jnp.where(lower, qp, zero)
            qm_ref[2 * pair + 1, 0:rows, :] = jnp.where(lower, zero, qp)
        for h in range(FOX_HEADS):
            fqb_ref[h, 0:rows, :] = jnp.broadcast_to(fq[:, h:h + 1], (rows, LANES))
        m_ref[:, 0:rows, :] = jnp.full((FOX_HEADS, rows, LANES), NEG_BIG, F32)
        l_ref[:, 0:rows, :] = jnp.zeros((FOX_HEADS, rows, LANES), F32)
        acc_ref[:, 0:rows, :] = jnp.zeros((FOX_HEADS, rows, LANES), F32)

        def lanes_like(x, n):
            return x[:, 0:n] if n <= LANES else jnp.concatenate([x] * (n // LANES), axis=1)

        def chunk(kstart, klen, mask):
            for pair in range(FOX_HEADS // 2):
                prow = slice(pair * LANES, (pair + 1) * LANES)
                kc = kt_ref[0, prow, pl.ds(kstart, klen)]
                vc = vt_ref[0, prow, pl.ds(kstart, klen)]
                for h in (2 * pair, 2 * pair + 1):
                    fk = ft_ref[h:h + 1, pl.ds(kstart, klen)]
                    s = _dot(qm_ref[h, 0:rows, :], kc) + (lanes_like(fqb_ref[h, 0:rows, :], klen) - fk)
                    if mask is not None:
                        s = jnp.where(mask, s, NEG_BIG)
                    m_old = m_ref[h, 0:rows, :]
                    m_new = jnp.maximum(m_old, jnp.max(s, axis=-1, keepdims=True))
                    alpha = jnp.exp(m_old - m_new)
                    p = jnp.exp(s - lanes_like(m_new, klen))
                    l_ref[h, 0:rows, :] = alpha * l_ref[h, 0:rows, :] + jnp.sum(p, axis=-1, keepdims=True)
                    m_ref[h, 0:rows, :] = m_new
                    acc_ref[h, 0:rows, :] = acc_ref[h, 0:rows, :] * alpha + _dot_nt(p.astype(BF16), vc)

        if isinstance(n_before, int):
            per_pass = max(1, (32 * 8 * LANES) // (rows * blk))
            for c0 in range(0, n_before, per_pass):
                chunk(c0 * blk, min(per_pass, n_before - c0) * blk, None)
        else:
            def body(c, carry):
                chunk(pl.multiple_of(c * blk, blk), blk, None)
                return carry

            lax.fori_loop(0, n_before, body, 0)
        causal = _iota2((rows, diag_len), 1) <= _iota2((rows, diag_len), 0)
        chunk(diag_start, diag_len, causal)
        for pair in range(FOX_HEADS // 2):
            o0 = acc_ref[2 * pair, 0:rows, :] * (1.0 / l_ref[2 * pair, 0:rows, :])
            o1 = acc_ref[2 * pair + 1, 0:rows, :] * (1.0 / l_ref[2 * pair + 1, 0:rows, :])
            o_ref[0, 0:rows, pair * LANES:(pair + 1) * LANES] = jnp.where(lower, o0, o1).astype(o_ref.dtype)

    if n_full:
        @pl.when(qi < n_full)
        def _():
            start = pl.multiple_of(qi * blk, blk)
            attend(blk, start, qi, start, blk)

    if tail:
        @pl.when(qi == n_full)
        def _():
            attend(tail, n_full * blk, n_full, n_full * blk, tail)


def fox_flash(q, ktb, vtb, lft):
    b, t, d = q.shape
    nq = pl.cdiv(t, FLASH_BLOCK)
    t_pad = nq * FLASH_BLOCK
    whole = lambda r: pl.BlockSpec((1, r, t), lambda bi, i: (bi, 0, 0))
    head_rows = (FOX_HEADS, FLASH_BLOCK, LANES)
    return pl.pallas_call(
        functools.partial(_flash_kernel, seq_len=t),
        grid=(b, nq),
        in_specs=[pl.BlockSpec((1, FLASH_BLOCK, d), lambda bi, i: (bi, i, 0)),
                  whole(d), whole(d), whole(FOX_HEADS)],
        out_specs=pl.BlockSpec((1, FLASH_BLOCK, d), lambda bi, i: (bi, i, 0)),
        out_shape=jax.ShapeDtypeStruct((b, t, d), BF16),
        scratch_shapes=[pltpu.VMEM((FOX_HEADS, t_pad), F32), pltpu.VMEM((t_pad, FOX_HEADS), F32),
                        pltpu.VMEM(head_rows, BF16), pltpu.VMEM(head_rows, F32),
                        pltpu.VMEM(head_rows, F32), pltpu.VMEM(head_rows, F32),
                        pltpu.VMEM(head_rows, F32)],
        compiler_params=_cparams("parallel", "arbitrary"),
        name="fox_flash",
    )(q, ktb, vtb, lft)


def _sample_attn_kernel(pt_ref, q_ref, lftn_ref, kn_ref, vn_ref, *refs, pages_per_step, n_new):
    pp = pages_per_step
    ck_refs, cv_refs, clf_refs = refs[0:pp], refs[pp:2 * pp], refs[2 * pp:3 * pp]
    o_ref, qbd_ref, fqb_ref, nqt_ref, carry_ref, m_ref, l_ref, acc_ref = refs[3 * pp:]
    step = pl.program_id(1)
    rows = n_new * FOX_HEADS
    diag = (_iota2((rows, D_MODEL), 0) % FOX_HEADS) == (_iota2((rows, D_MODEL), 1) // FOX_HEAD_DIM)

    @pl.when(step == 0)
    def _():
        q = q_ref[0]
        qrep = jnp.concatenate(
            [jnp.broadcast_to(q[i:i + 1, :], (FOX_HEADS, D_MODEL)) for i in range(n_new)], axis=0)
        qbd_ref[...] = jnp.where(diag, qrep, 0.0).astype(BF16)
        triu = (_iota2((n_new, n_new), 0) <= _iota2((n_new, n_new), 1)).astype(BF16)
        nqt = _dot3_l(lftn_ref[0], triu)
        nqt_ref[...] = nqt
        fqb_ref[...] = jnp.concatenate(
            [jnp.broadcast_to(nqt[:, i:i + 1], (FOX_HEADS, LANES)) for i in range(n_new)], axis=0)
        carry_ref[...] = jnp.zeros_like(carry_ref)
        m_ref[...] = jnp.full_like(m_ref, NEG_BIG)
        l_ref[...] = jnp.zeros_like(l_ref)
        acc_ref[...] = jnp.zeros_like(acc_ref)

    def rescale(s):
        m_old = m_ref[...]
        m_new = jnp.maximum(m_old, jnp.max(s, axis=-1, keepdims=True))
        alpha = jnp.exp(m_old - m_new)
        pr = jnp.exp(s - m_new)
        l_ref[...] = alpha * l_ref[...] + jnp.sum(pr, axis=-1, keepdims=True)
        m_ref[...] = m_new
        return alpha, pr.astype(BF16)

    after = (_iota2((PAGE_SIZE, PAGE_SIZE), 0) > _iota2((PAGE_SIZE, PAGE_SIZE), 1)).astype(BF16)
    ones = jnp.ones((PAGE_SIZE, PAGE_SIZE), BF16)
    carry = carry_ref[...]
    parts = []
    for i in range(pp):
        lft = clf_refs[i][0, 0]
        later = carry + _dot3_l(lft, after)
        carry = carry + _dot3_l(lft, ones)
        kt = ck_refs[i][0, 0].reshape(D_MODEL, PAGE_SIZE).astype(BF16)
        parts.append(_dot(qbd_ref[...], kt) + (fqb_ref[...] + jnp.concatenate([later] * n_new, axis=0)))
    carry_ref[...] = carry
    alpha, pr = rescale(jnp.concatenate(parts, axis=1))
    pv = None
    for i in range(pp):
        vt = cv_refs[i][0, 0].reshape(D_MODEL, PAGE_SIZE).astype(BF16)
        term = _dot_nt(pr[:, i * PAGE_SIZE:(i + 1) * PAGE_SIZE], vt)
        pv = term if pv is None else pv + term
    acc_ref[...] = acc_ref[...] * alpha + pv

    @pl.when(step == pl.num_programs(1) - 1)
    def _():
        nqt = nqt_ref[...]
        s_new = _dot_nt(qbd_ref[...], kn_ref[0].astype(BF16)) + (
            fqb_ref[:, 0:n_new] - jnp.concatenate([nqt] * n_new, axis=0))
        causal = _iota2((rows, n_new), 1) <= _iota2((rows, n_new), 0) // FOX_HEADS
        alpha, pr = rescale(jnp.where(causal, s_new, NEG_BIG))
        acc = acc_ref[...] * alpha + _dot(pr, vn_ref[0].astype(BF16))
        out = jnp.where(diag, acc * (1.0 / l_ref[...]), 0.0).astype(BF16)
        pick = (_iota2((n_new, rows), 1) // FOX_HEADS == _iota2((n_new, rows), 0)).astype(BF16)
        o_ref[0] = _dot(pick, out).astype(o_ref.dtype)


def sample_attn(page_table_flat, q, lft_new, k_new, v_new, ckt, cvt, clft, layer, n_pages):
    b, n_new, d = q.shape
    rows = n_new * FOX_HEADS
    pp = _tile(n_pages, (4, 2, 1))
    seq = lambda shape: pl.BlockSpec((1,) + shape, lambda bi, s, pt: (bi, 0, 0))

    def page(i, tail_shape):
        def index_map(bi, s, pt):
            phys = pt[bi * n_pages + (n_pages - 1 - (s * pp + i))]
            return (layer, phys) + (0,) * len(tail_shape)
        return pl.BlockSpec((1, 1) + tail_shape, index_map)

    kv_shape = (FOX_HEADS, FOX_HEAD_DIM, PAGE_SIZE)
    grid_spec = pltpu.PrefetchScalarGridSpec(
        num_scalar_prefetch=1,
        grid=(b, n_pages // pp),
        in_specs=[seq((n_new, d)), seq((FOX_HEADS, n_new)), seq((n_new, d)), seq((n_new, d))]
        + [page(i, kv_shape) for i in range(pp)] + [page(i, kv_shape) for i in range(pp)]
        + [page(i, (FOX_HEADS, PAGE_SIZE)) for i in range(pp)],
        out_specs=seq((n_new, d)),
        scratch_shapes=[pltpu.VMEM((rows, d), BF16), pltpu.VMEM((rows, LANES), F32),
                        pltpu.VMEM((FOX_HEADS, n_new), F32), pltpu.VMEM((FOX_HEADS, PAGE_SIZE), F32),
                        pltpu.VMEM((rows, 1), F32), pltpu.VMEM((rows, 1), F32),
                        pltpu.VMEM((rows, d), F32)],
    )
    return pl.pallas_call(
        functools.partial(_sample_attn_kernel, pages_per_step=pp, n_new=n_new),
        grid_spec=grid_spec,
        out_shape=jax.ShapeDtypeStruct((b, n_new, d), BF16),
        compiler_params=_cparams("parallel", "arbitrary"),
        name="sample_attn",
    )(page_table_flat, q, lft_new, k_new, v_new, *([ckt] * pp), *([cvt] * pp), *([clft] * pp))


CONV_PAD = 8


def _ssd_scan_kernel(xbc_ref, z_ref, dt_ref, h0_ref, ctx_ref, cw_ref, cb_ref, alog_ref, dexp_ref,
                     nw_ref, e_ref, et_ref, g_ref, hout_ref, cout_ref, h_ref, ext_ref,
                     *, seq_len, chunk):
    c = pl.program_id(1)
    n_chunks = pl.cdiv(seq_len, chunk)
    last_valid = seq_len - (n_chunks - 1) * chunk
    tail0 = CONV_PAD - (SSD_CONV - 1)

    @pl.when(c == 0)
    def _():
        h_ref[...] = h0_ref[0]
        ext_ref[tail0:CONV_PAD, :] = ctx_ref[0]

    ext_ref[CONV_PAD:CONV_PAD + chunk, :] = xbc_ref[0]
    cw = cw_ref[...]
    xc = cb_ref[...]
    for w in range(SSD_CONV):
        xc = xc + cw[w:w + 1, :] * ext_ref[tail0 + w:tail0 + w + chunk, :]
    xc = _silu(xc)
    dt = dt_ref[0]
    if last_valid != chunk:
        n_valid = jnp.where(c == n_chunks - 1, last_valid, chunk)
        xc = jnp.where(_iota2(xc.shape, 0) < n_valid, xc, 0.0)
        dt = jnp.where(_iota2(dt.shape, 0) < n_valid, dt, 0.0)
    xs = xc[:, :SSD_D_INNER]
    b_all = xc[:, SSD_D_INNER:SSD_D_INNER + SSD_GROUPS * SSD_STATE].astype(BF16)
    c_all = xc[:, SSD_D_INNER + SSD_GROUPS * SSD_STATE:].astype(BF16)

    a_neg = -jnp.exp(alog_ref[...])
    tril = (_iota2((chunk, chunk), 0) >= _iota2((chunk, chunk), 1))
    a_cum = _dot3_r(tril.astype(BF16), dt * a_neg)
    eye = (_iota2((SSD_HEADS, SSD_HEADS), 0) == _iota2((SSD_HEADS, SSD_HEADS), 1)).astype(BF16)
    a_cum_t = _dot3_nt_r(eye, a_cum)
    expand = e_ref[...]
    dt_e = _dot3_l(dt, expand)
    ac_e = _dot3_l(a_cum, expand)
    al_e = ac_e[chunk - 1:chunk, :]
    xdt = xs * dt_e
    xdt_b = xdt.astype(BF16)
    x_state = (xdt * jnp.exp(al_e - ac_e)).astype(BF16)
    ea_e = jnp.exp(ac_e)
    a_last_rows = jnp.broadcast_to(a_cum_t[:, chunk - 1:chunk], (SSD_HEADS, SSD_STATE))
    decay_rows = jnp.exp(_dot3_r(et_ref[...], a_last_rows))

    lower = _iota2((chunk, LANES), 1) < SSD_HEAD_DIM
    y_groups = []
    for g in range(SSD_GROUPS):
        gcols = slice(g * SSD_GROUP_WIDTH, (g + 1) * SSD_GROUP_WIDTH)
        cg = c_all[:, g * SSD_STATE:(g + 1) * SSD_STATE]
        bg = b_all[:, g * SSD_STATE:(g + 1) * SSD_STATE]
        cb = _dot_nt(cg, bg)
        h_old = h_ref[gcols, :]
        y_off = _dot_nt(cg, h_old.astype(BF16)) * ea_e[:, gcols]
        pairs = []
        for pr in range(SSD_HPG // 2):
            pcols = slice(g * SSD_GROUP_WIDTH + pr * LANES, g * SSD_GROUP_WIDTH + (pr + 1) * LANES)
            halves = []
            for hh in range(2):
                head = g * SSD_HPG + 2 * pr + hh
                diff = a_cum[:, head:head + 1] - a_cum_t[head:head + 1, :]
                lmat = jnp.where(tril, jnp.exp(diff), 0.0)
                halves.append(_dot((cb * lmat).astype(BF16), xdt_b[:, pcols]))
            pairs.append(jnp.where(lower, halves[0], halves[1]))
        y_groups.append(jnp.concatenate(pairs, axis=1) + y_off)
        h_ref[gcols, :] = decay_rows[gcols, :] * h_old + _dot_tn(x_state[:, gcols], bg)

    y = jnp.concatenate(y_groups, axis=1) + dexp_ref[...] * xs
    gated = y * _silu(z_ref[0])
    normed = []
    for g in range(SSD_GROUPS):
        gg = gated[:, g * SSD_GROUP_WIDTH:(g + 1) * SSD_GROUP_WIDTH]
        normed.append(gg * lax.rsqrt(jnp.mean(gg * gg, axis=-1, keepdims=True) + RMS_EPS))
    g_ref[0] = (jnp.concatenate(normed, axis=1) * nw_ref[...]).astype(g_ref.dtype)

    if n_chunks > 1:
        @pl.when(c < n_chunks - 1)
        def _():
            ext_ref[tail0:CONV_PAD, :] = ext_ref[tail0 + chunk:CONV_PAD + chunk, :]

    @pl.when(c == n_chunks - 1)
    def _():
        hout_ref[0] = h_ref[...]
        cout_ref[0] = ext_ref[tail0 + last_valid:CONV_PAD + last_valid, :]


def ssd_scan(xbc, z, dt, h0, ctx, conv_w, conv_b, a_log, d_exp, norm_w, expand, expand_t, chunk):
    b, t, _ = xbc.shape
    n_chunks = pl.cdiv(t, chunk)
    tok = lambda w: pl.BlockSpec((1, chunk, w), lambda bi, c: (bi, c, 0))
    per_seq = lambda shape: pl.BlockSpec((1,) + shape, lambda bi, c: (bi, 0, 0))
    return pl.pallas_call(
        functools.partial(_ssd_scan_kernel, seq_len=t, chunk=chunk),
        grid=(b, n_chunks),
        in_specs=[tok(SSD_CONV_DIM), tok(SSD_D_INNER), tok(SSD_HEADS),
                  per_seq((SSD_D_INNER, SSD_STATE)), per_seq((SSD_CONV - 1, SSD_CONV_DIM)),
                  _const_spec((SSD_CONV, SSD_CONV_DIM)), _const_spec((1, SSD_CONV_DIM)),
                  _const_spec((1, SSD_HEADS)), _const_spec((1, SSD_D_INNER)),
                  _const_spec((1, SSD_D_INNER)), _const_spec((SSD_HEADS, SSD_D_INNER)),
                  _const_spec((SSD_D_INNER, SSD_HEADS))],
        out_specs=[tok(SSD_D_INNER), per_seq((SSD_D_INNER, SSD_STATE)),
                   per_seq((SSD_CONV - 1, SSD_CONV_DIM))],
        out_shape=[jax.ShapeDtypeStruct((b, t, SSD_D_INNER), BF16),
                   jax.ShapeDtypeStruct((b, SSD_D_INNER, SSD_STATE), F32),
                   jax.ShapeDtypeStruct((b, SSD_CONV - 1, SSD_CONV_DIM), F32)],
        scratch_shapes=[pltpu.VMEM((SSD_D_INNER, SSD_STATE), F32),
                        pltpu.VMEM((CONV_PAD + chunk, SSD_CONV_DIM), F32)],
        compiler_params=_cparams("parallel", "arbitrary"),
        name="ssd_scan",
    )(xbc, z, dt, h0, ctx, conv_w, conv_b, a_log, d_exp, norm_w, expand, expand_t)


def kernel(x_prompt, x_sample, cache_k, cache_v, cache_lf, state_ssm, state_conv, page_table,
           meta_tokens, norm_mix, norm_ffn, norm_final, fox_w_in, fox_b_f, fox_w_out,
           ssd_w_in, ssd_conv_w, ssd_conv_b, ssd_dt_bias, ssd_a_log, ssd_d, ssd_norm, ssd_w_out,
           ffn_w_gate, ffn_w_up, ffn_w_down):
    bp, seq, d = x_prompt.shape
    bs, ls, _ = x_sample.shape
    depth = norm_mix.shape[0]
    t = seq + N_META
    n_pages = page_table.shape[1]
    pt_flat = page_table.reshape(-1).astype(jnp.int32)
    ckt = jnp.transpose(cache_k, (0, 1, 3, 4, 2))
    cvt = jnp.transpose(cache_v, (0, 1, 3, 4, 2))
    clft = jnp.transpose(cache_lf, (0, 1, 3, 2))

    meta = jnp.broadcast_to(meta_tokens[None].astype(x_prompt.dtype), (bp, N_META, d))
    hp = jnp.concatenate([meta, x_prompt], axis=1).reshape(bp * t, d)
    hs = x_sample.reshape(bs * ls, d)

    heads = jnp.arange(SSD_D_INNER, dtype=jnp.int32) // SSD_HEAD_DIM
    expand = (heads[None, :] == jnp.arange(SSD_HEADS, dtype=jnp.int32)[:, None]).astype(BF16)
    expand_t = expand.T
    row = lambda v: v.reshape(1, -1)
    sample_chunk = math.gcd(ls, SSD_CHUNK)

    outs = {n: [] for n in ("kp", "vp", "lfp", "ks", "vs", "lfs", "ssm_p", "conv_p", "ssm_s", "conv_s")}
    for i in range(depth):
        j = i // 2
        g_mix = row(norm_mix[i])
        if i % 2 == 0:
            w_in = fox_w_in[j].astype(BF16)
            wq, wk, wv = w_in[:, :d], w_in[:, d:2 * d], w_in[:, 2 * d:3 * d]
            wf = w_in[:, 3 * d:]
            w_out = fox_w_out[j].astype(BF16)
            q, kt, vt, ktb, vtb, lft = fox_in_prompt(
                hp.reshape(bp, t, d), g_mix, wq, wk.T, wv.T, wf.T, fox_b_f[j].reshape(FOX_HEADS, 1))
            o = fox_flash(q, ktb, vtb, lft)
            hp = matmul_res(o.reshape(bp * t, d), w_out, hp)
            to_heads = lambda a: jnp.transpose(a.reshape(bp, FOX_HEADS, FOX_HEAD_DIM, t), (0, 3, 1, 2))
            outs["kp"].append(to_heads(kt))
            outs["vp"].append(to_heads(vt))
            outs["lfp"].append(jnp.transpose(lft, (0, 2, 1)))
            q, k, v, lf = fox_in_sample(hs, g_mix, wq, wk, wv, wf, row(fox_b_f[j]))
            lf3 = lf.reshape(bs, ls, FOX_HEADS)
            o = sample_attn(pt_flat, q.reshape(bs, ls, d), jnp.transpose(lf3, (0, 2, 1)),
                            k.reshape(bs, ls, d), v.reshape(bs, ls, d), ckt, cvt, clft, j, n_pages)
            hs = matmul_res(o.reshape(bs * ls, d), w_out, hs)
            outs["ks"].append(k.reshape(bs, ls, FOX_HEADS, FOX_HEAD_DIM))
            outs["vs"].append(v.reshape(bs, ls, FOX_HEADS, FOX_HEAD_DIM))
            outs["lfs"].append(lf3)
        else:
            w_in = ssd_w_in[j].astype(BF16)
            wz = w_in[:, :SSD_D_INNER]
            wx = w_in[:, SSD_D_INNER:SSD_D_INNER + SSD_CONV_DIM]
            wdt = w_in[:, SSD_D_INNER + SSD_CONV_DIM:]
            w_out = ssd_w_out[j].astype(BF16)
            d_exp = row(jnp.repeat(ssd_d[j].astype(F32), SSD_HEAD_DIM))
            shared = (ssd_conv_w[j], row(ssd_conv_b[j]), row(ssd_a_log[j]), d_exp, row(ssd_norm[j]),
                      expand, expand_t)
            z, xbc, dt = ssd_in(hp, g_mix, wz, wx, wdt, row(ssd_dt_bias[j]))
            gated, ssm, conv = ssd_scan(
                xbc.reshape(bp, t, SSD_CONV_DIM), z.reshape(bp, t, SSD_D_INNER),
                dt.reshape(bp, t, SSD_HEADS),
                jnp.zeros((bp, SSD_D_INNER, SSD_STATE), F32),
                jnp.zeros((bp, SSD_CONV - 1, SSD_CONV_DIM), F32), *shared, SSD_CHUNK)
            hp = matmul_res(gated.reshape(bp * t, SSD_D_INNER), w_out, hp)
            outs["ssm_p"].append(ssm.reshape(bp, SSD_HEADS, SSD_HEAD_DIM, SSD_STATE))
            outs["conv_p"].append(conv)
            z, xbc, dt = ssd_in(hs, g_mix, wz, wx, wdt, row(ssd_dt_bias[j]))
            gated, ssm, conv = ssd_scan(
                xbc.reshape(bs, ls, SSD_CONV_DIM), z.reshape(bs, ls, SSD_D_INNER),
                dt.reshape(bs, ls, SSD_HEADS),
                state_ssm[j].reshape(bs, SSD_D_INNER, SSD_STATE), state_conv[j], *shared, sample_chunk)
            hs = matmul_res(gated.reshape(bs * ls, SSD_D_INNER), w_out, hs)
            outs["ssm_s"].append(ssm.reshape(bs, SSD_HEADS, SSD_HEAD_DIM, SSD_STATE))
            outs["conv_s"].append(conv)
        last = i == depth - 1
        ffn_w = (row(norm_ffn[i]), ffn_w_gate[i].astype(BF16), ffn_w_up[i].astype(BF16),
                 ffn_w_down[i].astype(BF16), row(norm_final))
        hp = ffn(hp, *ffn_w, last)
        hs = ffn(hs, *ffn_w, last)

    y_prompt = hp.reshape(bp, t, d)[:, N_META:]
    y_sample = hs.reshape(bs, ls, d)
    st = lambda n: jnp.stack(outs[n])
    return (y_prompt, y_sample, st("kp"), st("vp"), st("lfp"), st("ssm_p"), st("conv_p"),
            st("ks"), st("vs"), st("lfs"), st("ssm_s"), st("conv_s"))
```

```python
import functools
import math

import jax
import jax.numpy as jnp
from jax import lax
from jax.experimental import pallas as pl
from jax.experimental.pallas import tpu as pltpu

F32 = jnp.float32
BF16 = jnp.bfloat16

D_MODEL = 1024
N_META = 16
FOX_HEADS = 16
FOX_HEAD_DIM = 64
PAGE_SIZE = 128
SSD_D_INNER = 2048
SSD_HEAD_DIM = 64
SSD_HEADS = 32
SSD_GROUPS = 4
SSD_HPG = 8
SSD_STATE = 128
SSD_CONV = 4
SSD_CONV_DIM = 3072
SSD_CHUNK = 128
SSD_GROUP_WIDTH = SSD_HPG * SSD_HEAD_DIM
FFN_HIDDEN = 2816
RMS_EPS = 1e-6
NEG_BIG = -1e30
LOG2E = math.log2(math.e)

LANES = 128
VMEM_LIMIT = 56 * 1024 * 1024
FLASH_BLOCK = 256
FOX_IN_TILE = 384
FFN_CHUNK = 256
SAMPLE_PAGES_PER_STEP = 16


def _cparams(*sem):
    return pltpu.CompilerParams(dimension_semantics=sem, vmem_limit_bytes=VMEM_LIMIT)


def _dot(a, b):
    return jnp.dot(a, b, preferred_element_type=F32)


def _dot_nt(a, b):
    return lax.dot_general(a, b, (((1,), (1,)), ((), ())), preferred_element_type=F32)


def _dot_tn(a, b):
    return lax.dot_general(a, b, (((0,), (0,)), ((), ())), preferred_element_type=F32)


def _split3(x):
    hi = x.astype(BF16)
    r1 = x - hi.astype(F32)
    mid = r1.astype(BF16)
    lo = (r1 - mid.astype(F32)).astype(BF16)
    return hi, mid, lo


def _dot3_r(a_exact, x):
    hi, mid, lo = _split3(x)
    return _dot(a_exact, lo) + _dot(a_exact, mid) + _dot(a_exact, hi)


def _dot3_l(x, b_exact):
    hi, mid, lo = _split3(x)
    return _dot(lo, b_exact) + _dot(mid, b_exact) + _dot(hi, b_exact)


def _dot3_nt_r(a_exact, x):
    hi, mid, lo = _split3(x)
    return _dot_nt(a_exact, lo) + _dot_nt(a_exact, mid) + _dot_nt(a_exact, hi)


def _dot3_l_stacked(x, b_exact_x3):
    return _dot(jnp.concatenate(_split3(x), axis=1), b_exact_x3)


def _dot3_r_stacked(a_exact_x3, x):
    return _dot(a_exact_x3, jnp.concatenate(_split3(x), axis=0))


def _rms(x, g):
    ms = jnp.mean(x * x, axis=-1, keepdims=True)
    return x * lax.rsqrt(ms + RMS_EPS) * g


def _log_sigmoid(x):
    return jnp.minimum(x, 0.0) - jnp.log1p(jnp.exp(-jnp.abs(x)))


def _softplus(x):
    return jnp.maximum(x, 0.0) + jnp.log1p(jnp.exp(-jnp.abs(x)))


def _silu(x):
    return x * (1.0 / (1.0 + jnp.exp(-x)))


def _iota2(shape, dim):
    return lax.broadcasted_iota(jnp.int32, shape, dim)


def _tile(m, candidates):
    for c in candidates:
        if m % c == 0:
            return c
    return m


def _const_spec(shape):
    nd = len(shape)
    return pl.BlockSpec(shape, lambda *_: (0,) * nd)


def _ssd_in_kernel(x_ref, g_ref, wz_ref, wx_ref, wdt_ref, dtb_ref, z_ref, xbc_ref, dt_ref):
    xn = _rms(x_ref[...], g_ref[...]).astype(BF16)
    z_ref[...] = _dot(xn, wz_ref[...])
    xbc_ref[...] = _dot(xn, wx_ref[...])
    dt_ref[...] = _softplus(_dot(xn, wdt_ref[...]) + dtb_ref[...])


def ssd_in(x, g, wz, wx, wdt, dt_bias):
    m = x.shape[0]
    tm = _tile(m, (384, 256, 128, 144, 96, 48, 16))
    row = lambda w: pl.BlockSpec((tm, w), lambda i: (i, 0))
    return pl.pallas_call(
        _ssd_in_kernel,
        grid=(m // tm,),
        in_specs=[row(D_MODEL), _const_spec((1, D_MODEL)),
                  _const_spec((D_MODEL, SSD_D_INNER)), _const_spec((D_MODEL, SSD_CONV_DIM)),
                  _const_spec((D_MODEL, SSD_HEADS)), _const_spec((1, SSD_HEADS))],
        out_specs=[row(SSD_D_INNER), row(SSD_CONV_DIM), row(SSD_HEADS)],
        out_shape=[jax.ShapeDtypeStruct((m, SSD_D_INNER), F32),
                   jax.ShapeDtypeStruct((m, SSD_CONV_DIM), F32),
                   jax.ShapeDtypeStruct((m, SSD_HEADS), F32)],
        compiler_params=_cparams("parallel"),
        name="ssd_in",
    )(x, g, wz, wx, wdt, dt_bias)


def _matmul_res_kernel(a_ref, w_ref, r_ref, o_ref):
    o_ref[...] = r_ref[...] + _dot(a_ref[...], w_ref[...])


def matmul_res(a, w, res):
    m, k = a.shape
    n = w.shape[1]
    tm = _tile(m, (384, 256, 128, 144, 96, 48, 16))
    return pl.pallas_call(
        _matmul_res_kernel,
        grid=(m // tm,),
        in_specs=[pl.BlockSpec((tm, k), lambda i: (i, 0)), _const_spec((k, n)),
                  pl.BlockSpec((tm, n), lambda i: (i, 0))],
        out_specs=pl.BlockSpec((tm, n), lambda i: (i, 0)),
        out_shape=jax.ShapeDtypeStruct((m, n), F32),
        compiler_params=_cparams("parallel"),
        name="matmul_res",
    )(a, w, res)


def _ffn_kernel(x_ref, g_ref, wg_ref, wu_ref, wd_ref, gf_ref, o_ref, acc_ref, *, final_norm):
    x = x_ref[...]
    xn = _rms(x, g_ref[...]).astype(BF16)
    acc_ref[...] = x
    for c in range(FFN_HIDDEN // FFN_CHUNK):
        cols = slice(c * FFN_CHUNK, (c + 1) * FFN_CHUNK)
        gate = _dot(xn, wg_ref[0, :, cols])
        up = _dot(xn, wu_ref[0, :, cols])
        act = (_silu(gate) * up).astype(BF16)
        acc_ref[...] += _dot(act, wd_ref[0, cols, :])
    out = acc_ref[...]
    o_ref[...] = _rms(out, gf_ref[...]) if final_norm else out


def ffn(x, g, wg, wu, wd, layer, g_final, final_norm):
    m = x.shape[0]
    tm = _tile(m, (688, 512, 256, 128, 144, 96, 48, 16))
    weight = lambda r, c: pl.BlockSpec((1, r, c), lambda i: (layer, 0, 0))
    return pl.pallas_call(
        functools.partial(_ffn_kernel, final_norm=final_norm),
        grid=(m // tm,),
        in_specs=[pl.BlockSpec((tm, D_MODEL), lambda i: (i, 0)), _const_spec((1, D_MODEL)),
                  weight(D_MODEL, FFN_HIDDEN), weight(D_MODEL, FFN_HIDDEN),
                  weight(FFN_HIDDEN, D_MODEL), _const_spec((1, D_MODEL))],
        out_specs=pl.BlockSpec((tm, D_MODEL), lambda i: (i, 0)),
        out_shape=jax.ShapeDtypeStruct((m, D_MODEL), F32),
        scratch_shapes=[pltpu.VMEM((tm, D_MODEL), F32)],
        compiler_params=_cparams("parallel"),
        name="ffn",
    )(x, g, wg, wu, wd, g_final)


def _fox_in_prompt_kernel(x_ref, g_ref, wq_ref, wkt_ref, wvt_ref, wft_ref, bft_ref, *refs, n_prev):
    prev_k_ref, prev_v_ref = refs[:2] if n_prev else (None, None)
    q_ref, kt_ref, vt_ref, ktb_ref, vtb_ref, lft_ref = refs[2 if n_prev else 0:]
    xn = _rms(x_ref[0], g_ref[...]).astype(BF16)
    q_ref[0] = (_dot(xn, wq_ref[...]) * (FOX_HEAD_DIM ** -0.5 * LOG2E)).astype(BF16)
    kt = _dot_nt(wkt_ref[...], xn)
    kt_ref[n_prev, 0] = kt
    ktb_ref[0] = kt.astype(BF16)
    vt = _dot_nt(wvt_ref[...], xn)
    vt_ref[n_prev, 0] = vt
    vtb_ref[0] = vt.astype(BF16)
    lft_ref[0] = _log_sigmoid(_dot_nt(wft_ref[...], xn) + bft_ref[...])
    for layer in range(n_prev):
        kt_ref[layer, 0] = prev_k_ref[layer, 0]
        vt_ref[layer, 0] = prev_v_ref[layer, 0]


def fox_in_prompt(x, g, wq, wkt, wvt, wft, b_ft, prev_kt=None, prev_vt=None):
    b, t, d = x.shape
    n_prev = 0 if prev_kt is None else prev_kt.shape[0]
    tt = FOX_IN_TILE if t > FOX_IN_TILE else t
    rows = pl.BlockSpec((1, tt, d), lambda bi, i: (bi, i, 0))
    cols = lambda r: pl.BlockSpec((1, r, tt), lambda bi, i: (bi, 0, i))
    stack = lambda n: pl.BlockSpec((n, 1, d, tt), lambda bi, i: (0, bi, 0, i))
    prev = [prev_kt, prev_vt] if n_prev else []
    return pl.pallas_call(
        functools.partial(_fox_in_prompt_kernel, n_prev=n_prev),
        grid=(b, pl.cdiv(t, tt)),
        in_specs=[rows, _const_spec((1, d)), _const_spec((d, d)), _const_spec((d, d)),
                  _const_spec((d, d)), _const_spec((FOX_HEADS, d)), _const_spec((FOX_HEADS, 1))]
        + [stack(n_prev)] * len(prev),
        out_specs=[rows, stack(n_prev + 1), stack(n_prev + 1), cols(d), cols(d), cols(FOX_HEADS)],
        out_shape=[jax.ShapeDtypeStruct((b, t, d), BF16),
                   jax.ShapeDtypeStruct((n_prev + 1, b, d, t), F32),
                   jax.ShapeDtypeStruct((n_prev + 1, b, d, t), F32),
                   jax.ShapeDtypeStruct((b, d, t), BF16), jax.ShapeDtypeStruct((b, d, t), BF16),
                   jax.ShapeDtypeStruct((b, FOX_HEADS, t), F32)],
        compiler_params=_cparams("parallel", "parallel"),
        name="fox_in_prompt",
    )(x, g, wq, wkt, wvt, wft, b_ft, *prev)


def _fox_in_sample_kernel(x_ref, g_ref, wq_ref, wk_ref, wv_ref, wf_ref, bf_ref,
                          q_ref, k_ref, v_ref, lf_ref):
    xn = _rms(x_ref[...], g_ref[...]).astype(BF16)
    q_ref[...] = _dot(xn, wq_ref[...]) * (FOX_HEAD_DIM ** -0.5)
    k_ref[...] = _dot(xn, wk_ref[...])
    v_ref[...] = _dot(xn, wv_ref[...])
    lf_ref[...] = _log_sigmoid(_dot(xn, wf_ref[...]) + bf_ref[...])


def fox_in_sample(x, g, wq, wk, wv, wf, b_f):
    m, d = x.shape
    tm = _tile(m, (256, 128, 64, 16))
    row = lambda w: pl.BlockSpec((tm, w), lambda i: (i, 0))
    return pl.pallas_call(
        _fox_in_sample_kernel,
        grid=(m // tm,),
        in_specs=[row(d), _const_spec((1, d)), _const_spec((d, d)), _const_spec((d, d)),
                  _const_spec((d, d)), _const_spec((d, FOX_HEADS)), _const_spec((1, FOX_HEADS))],
        out_specs=[row(d), row(d), row(d), row(FOX_HEADS)],
        out_shape=[jax.ShapeDtypeStruct((m, d), F32)] * 3 + [jax.ShapeDtypeStruct((m, FOX_HEADS), F32)],
        compiler_params=_cparams("parallel"),
        name="fox_in_sample",
    )(x, g, wq, wk, wv, wf, b_f)


def _flash_kernel(q_ref, kt_ref, vt_ref, lft_ref, o_ref,
                  ft_ref, f_ref, qm_ref, fqb_ref, m_ref, l_ref, acc_ref, *, seq_len):
    blk = FLASH_BLOCK
    n_full = seq_len // blk
    tail = seq_len % blk
    qi = pl.program_id(1)

    @pl.when(qi == 0)
    def _():
        carry = jnp.zeros((FOX_HEADS, 1), F32)
        for start in range(0, seq_len, LANES):
            w = min(LANES, seq_len - start)
            triu = (_iota2((w, w), 0) <= _iota2((w, w), 1)).astype(BF16)
            eye = (_iota2((w, w), 0) == _iota2((w, w), 1)).astype(BF16)
            ft = carry + _dot3_l(lft_ref[0, :, start:start + w], triu)
            carry = ft[:, w - 1:w]
            ft = ft * LOG2E
            ft_ref[:, start:start + w] = ft
            f_ref[start:start + w, :] = _dot3_nt_r(eye, ft)

    def attend(rows, row0, n_before, diag_start, diag_len):
        lower = _iota2((rows, LANES), 1) < FOX_HEAD_DIM
        fq = f_ref[pl.ds(row0, rows), :]
        for pair in range(FOX_HEADS // 2):
            qp = q_ref[0, 0:rows, pair * LANES:(pair + 1) * LANES]
            zero = jnp.zeros_like(qp)
            qm_ref[2 * pair, 0:rows, :] = jnp.where(lower, qp, zero)
            qm_ref[2 * pair + 1, 0:rows, :] = jnp.where(lower, zero, qp)
        for h in range(FOX_HEADS):
            fqb_ref[h, 0:rows, :] = jnp.broadcast_to(fq[:, h:h + 1], (rows, LANES))
        m_ref[:, 0:rows, :] = jnp.full((FOX_HEADS, rows, LANES), NEG_BIG, F32)
        l_ref[:, 0:rows, :] = jnp.zeros((FOX_HEADS, rows, LANES), F32)
        acc_ref[:, 0:rows, :] = jnp.zeros((FOX_HEADS, rows, LANES), F32)

        def lanes_like(x, n):
            return x[:, 0:n] if n <= LANES else jnp.concatenate([x] * (n // LANES), axis=1)

        def chunk(kstart, klen, mask):
            for pair in range(FOX_HEADS // 2):
                prow = slice(pair * LANES, (pair + 1) * LANES)
                kc = kt_ref[0, prow, pl.ds(kstart, klen)]
                vc = vt_ref[0, prow, pl.ds(kstart, klen)]
                for h in (2 * pair, 2 * pair + 1):
                    fk = ft_ref[h:h + 1, pl.ds(kstart, klen)]
                    s = _dot(qm_ref[h, 0:rows, :], kc) + (lanes_like(fqb_ref[h, 0:rows, :], klen) - fk)
                    if mask is not None:
                        s = jnp.where(mask, s, NEG_BIG)
                    m_old = m_ref[h, 0:rows, :]
                    m_new = jnp.maximum(m_old, jnp.max(s, axis=-1, keepdims=True))
                    alpha = jnp.exp2(m_old - m_new)
                    p = jnp.exp2(s - lanes_like(m_new, klen))
                    l_ref[h, 0:rows, :] = alpha * l_ref[h, 0:rows, :] + jnp.sum(p, axis=-1, keepdims=True)
                    m_ref[h, 0:rows, :] = m_new
                    acc_ref[h, 0:rows, :] = acc_ref[h, 0:rows, :] * alpha + _dot_nt(p.astype(BF16), vc)

        if isinstance(n_before, int):
            per_pass = max(1, (32 * 8 * LANES) // (rows * blk))
            for c0 in range(0, n_before, per_pass):
                chunk(c0 * blk, min(per_pass, n_before - c0) * blk, None)
        else:
            def body(c, carry):
                chunk(pl.multiple_of(c * blk, blk), blk, None)
                return carry

            lax.fori_loop(0, n_before, body, 0)
        causal = _iota2((rows, diag_len), 1) <= _iota2((rows, diag_len), 0)
        chunk(diag_start, diag_len, causal)
        for pair in range(FOX_HEADS // 2):
            o0 = acc_ref[2 * pair, 0:rows, :] * (1.0 / l_ref[2 * pair, 0:rows, :])
            o1 = acc_ref[2 * pair + 1, 0:rows, :] * (1.0 / l_ref[2 * pair + 1, 0:rows, :])
            o_ref[0, 0:rows, pair * LANES:(pair + 1) * LANES] = jnp.where(lower, o0, o1).astype(o_ref.dtype)

    if n_full:
        @pl.when(qi < n_full)
        def _():
            start = pl.multiple_of(qi * blk, blk)
            attend(blk, start, qi, start, blk)

    if tail:
        @pl.when(qi == n_full)
        def _():
            attend(tail, n_full * blk, n_full, n_full * blk, tail)


def fox_flash(q, ktb, vtb, lft):
    b, t, d = q.shape
    nq = pl.cdiv(t, FLASH_BLOCK)
    t_pad = nq * FLASH_BLOCK
    whole = lambda r: pl.BlockSpec((1, r, t), lambda bi, i: (bi, 0, 0))
    head_rows = (FOX_HEADS, FLASH_BLOCK, LANES)
    return pl.pallas_call(
        functools.partial(_flash_kernel, seq_len=t),
        grid=(b, nq),
        in_specs=[pl.BlockSpec((1, FLASH_BLOCK, d), lambda bi, i: (bi, i, 0)),
                  whole(d), whole(d), whole(FOX_HEADS)],
        out_specs=pl.BlockSpec((1, FLASH_BLOCK, d), lambda bi, i: (bi, i, 0)),
        out_shape=jax.ShapeDtypeStruct((b, t, d), BF16),
        scratch_shapes=[pltpu.VMEM((FOX_HEADS, t_pad), F32), pltpu.VMEM((t_pad, FOX_HEADS), F32),
                        pltpu.VMEM(head_rows, BF16), pltpu.VMEM(head_rows, F32),
                        pltpu.VMEM(head_rows, F32), pltpu.VMEM(head_rows, F32),
                        pltpu.VMEM(head_rows, F32)],
        compiler_params=_cparams("parallel", "arbitrary"),
        name="fox_flash",
    )(q, ktb, vtb, lft)


def _sample_attn_kernel(pt_ref, q_ref, lftn_ref, kn_ref, vn_ref, *refs, pages_per_step, n_new):
    pp = pages_per_step
    ck_refs, cv_refs, clf_refs = refs[0:pp], refs[pp:2 * pp], refs[2 * pp:3 * pp]
    o_ref, qbd_ref, fqb_ref, nqt_ref, carry_ref, m_ref, l_ref, acc_ref = refs[3 * pp:]
    step = pl.program_id(1)
    rows = n_new * FOX_HEADS
    diag = (_iota2((rows, D_MODEL), 0) % FOX_HEADS) == (_iota2((rows, D_MODEL), 1) // FOX_HEAD_DIM)

    @pl.when(step == 0)
    def _():
        q = q_ref[0]
        qrep = jnp.concatenate(
            [jnp.broadcast_to(q[i:i + 1, :], (FOX_HEADS, D_MODEL)) for i in range(n_new)], axis=0)
        qbd_ref[...] = jnp.where(diag, qrep, 0.0).astype(BF16)
        triu = (_iota2((n_new, n_new), 0) <= _iota2((n_new, n_new), 1)).astype(BF16)
        nqt = _dot3_l(lftn_ref[0], triu)
        nqt_ref[...] = nqt
        fqb_ref[...] = jnp.concatenate(
            [jnp.broadcast_to(nqt[:, i:i + 1], (FOX_HEADS, LANES)) for i in range(n_new)], axis=0)
        carry_ref[...] = jnp.zeros_like(carry_ref)
        m_ref[...] = jnp.full_like(m_ref, NEG_BIG)
        l_ref[...] = jnp.zeros_like(l_ref)
        acc_ref[...] = jnp.zeros_like(acc_ref)

    def rescale(s):
        m_old = m_ref[...]
        m_new = jnp.maximum(m_old, jnp.max(s, axis=-1, keepdims=True))
        alpha = jnp.exp(m_old - m_new)
        pr = jnp.exp(s - m_new)
        l_ref[...] = alpha * l_ref[...] + jnp.sum(pr, axis=-1, keepdims=True)
        m_ref[...] = m_new
        return alpha, pr.astype(BF16)

    after = (_iota2((PAGE_SIZE, PAGE_SIZE), 0) > _iota2((PAGE_SIZE, PAGE_SIZE), 1)).astype(BF16)
    ones = jnp.ones((PAGE_SIZE, PAGE_SIZE), BF16)
    carry = carry_ref[...]
    parts = []
    for i in range(pp):
        lft = clf_refs[i][0, 0]
        later = carry + _dot3_l(lft, after)
        carry = carry + _dot3_l(lft, ones)
        kt = ck_refs[i][0, 0].reshape(D_MODEL, PAGE_SIZE).astype(BF16)
        parts.append(_dot(qbd_ref[...], kt) + (fqb_ref[...] + jnp.concatenate([later] * n_new, axis=0)))
    carry_ref[...] = carry
    alpha, pr = rescale(jnp.concatenate(parts, axis=1))
    pv = None
    for i in range(pp):
        vt = cv_refs[i][0, 0].reshape(D_MODEL, PAGE_SIZE).astype(BF16)
        term = _dot_nt(pr[:, i * PAGE_SIZE:(i + 1) * PAGE_SIZE], vt)
        pv = term if pv is None else pv + term
    acc_ref[...] = acc_ref[...] * alpha + pv

    @pl.when(step == pl.num_programs(1) - 1)
    def _():
        nqt = nqt_ref[...]
        s_new = _dot_nt(qbd_ref[...], kn_ref[0].astype(BF16)) + (
            fqb_ref[:, 0:n_new] - jnp.concatenate([nqt] * n_new, axis=0))
        causal = _iota2((rows, n_new), 1) <= _iota2((rows, n_new), 0) // FOX_HEADS
        alpha, pr = rescale(jnp.where(causal, s_new, NEG_BIG))
        acc = acc_ref[...] * alpha + _dot(pr, vn_ref[0].astype(BF16))
        out = jnp.where(diag, acc * (1.0 / l_ref[...]), 0.0).astype(BF16)
        pick = (_iota2((n_new, rows), 1) // FOX_HEADS == _iota2((n_new, rows), 0)).astype(BF16)
        o_ref[0] = _dot(pick, out).astype(o_ref.dtype)


def sample_attn(page_table_flat, q, lft_new, k_new, v_new, ckt, cvt, clft, layer, n_pages):
    b, n_new, d = q.shape
    rows = n_new * FOX_HEADS
    pp = _tile(n_pages, (SAMPLE_PAGES_PER_STEP, 8, 4, 2, 1))
    seq = lambda shape: pl.BlockSpec((1,) + shape, lambda bi, s, pt: (bi, 0, 0))

    def page(i, tail_shape):
        def index_map(bi, s, pt):
            phys = pt[bi * n_pages + (n_pages - 1 - (s * pp + i))]
            return (layer, phys) + (0,) * len(tail_shape)
        return pl.BlockSpec((1, 1) + tail_shape, index_map)

    kv_shape = (FOX_HEADS, FOX_HEAD_DIM, PAGE_SIZE)
    grid_spec = pltpu.PrefetchScalarGridSpec(
        num_scalar_prefetch=1,
        grid=(b, n_pages // pp),
        in_specs=[seq((n_new, d)), seq((FOX_HEADS, n_new)), seq((n_new, d)), seq((n_new, d))]
        + [page(i, kv_shape) for i in range(pp)] + [page(i, kv_shape) for i in range(pp)]
        + [page(i, (FOX_HEADS, PAGE_SIZE)) for i in range(pp)],
        out_specs=seq((n_new, d)),
        scratch_shapes=[pltpu.VMEM((rows, d), BF16), pltpu.VMEM((rows, LANES), F32),
                        pltpu.VMEM((FOX_HEADS, n_new), F32), pltpu.VMEM((FOX_HEADS, PAGE_SIZE), F32),
                        pltpu.VMEM((rows, 1), F32), pltpu.VMEM((rows, 1), F32),
                        pltpu.VMEM((rows, d), F32)],
    )
    return pl.pallas_call(
        functools.partial(_sample_attn_kernel, pages_per_step=pp, n_new=n_new),
        grid_spec=grid_spec,
        out_shape=jax.ShapeDtypeStruct((b, n_new, d), BF16),
        compiler_params=_cparams("parallel", "arbitrary"),
        name="sample_attn",
    )(page_table_flat, q, lft_new, k_new, v_new, *([ckt] * pp), *([cvt] * pp), *([clft] * pp))


CONV_PAD = 8


def _ssd_scan_kernel(xbc_ref, z_ref, dt_ref, h0_ref, ctx_ref, cw_ref, cb_ref, alog_ref, dexp_ref,
                     nw_ref, e_ref, et_ref, *refs, seq_len, chunk, n_prev):
    prev_ref = refs[0] if n_prev else None
    g_ref, hout_ref, cout_ref, h_ref, ext_ref = refs[1 if n_prev else 0:]
    c = pl.program_id(1)
    n_chunks = pl.cdiv(seq_len, chunk)
    last_valid = seq_len - (n_chunks - 1) * chunk
    tail0 = CONV_PAD - (SSD_CONV - 1)

    @pl.when(c == 0)
    def _():
        h_ref[...] = h0_ref[0, 0]
        ext_ref[tail0:CONV_PAD, :] = ctx_ref[0]

    ext_ref[CONV_PAD:CONV_PAD + chunk, :] = xbc_ref[0]
    cw = cw_ref[...]
    xc = cb_ref[...]
    for w in range(SSD_CONV):
        xc = xc + cw[w:w + 1, :] * ext_ref[tail0 + w:tail0 + w + chunk, :]
    xc = _silu(xc)
    dt = dt_ref[0]
    if last_valid != chunk:
        n_valid = jnp.where(c == n_chunks - 1, last_valid, chunk)
        xc = jnp.where(_iota2(xc.shape, 0) < n_valid, xc, 0.0)
        dt = jnp.where(_iota2(dt.shape, 0) < n_valid, dt, 0.0)
    xs = xc[:, :SSD_D_INNER]
    b_all = xc[:, SSD_D_INNER:SSD_D_INNER + SSD_GROUPS * SSD_STATE].astype(BF16)
    c_all = xc[:, SSD_D_INNER + SSD_GROUPS * SSD_STATE:].astype(BF16)

    a_neg = -jnp.exp(alog_ref[...])
    tril = (_iota2((chunk, chunk), 0) >= _iota2((chunk, chunk), 1))
    a_cum = _dot3_r(tril.astype(BF16), dt * a_neg)
    eye = (_iota2((SSD_HEADS, SSD_HEADS), 0) == _iota2((SSD_HEADS, SSD_HEADS), 1)).astype(BF16)
    a_cum_t = _dot3_nt_r(eye, a_cum)
    expand3 = e_ref[...]
    dt_e = _dot3_l_stacked(dt, expand3)
    ac_e = _dot3_l_stacked(a_cum, expand3)
    al_e = ac_e[chunk - 1:chunk, :]
    xdt = xs * dt_e
    xdt_b = xdt.astype(BF16)
    x_state = (xdt * jnp.exp(al_e - ac_e)).astype(BF16)
    ea_e = jnp.exp(ac_e)
    a_last_rows = jnp.broadcast_to(a_cum_t[:, chunk - 1:chunk], (SSD_HEADS, SSD_STATE))
    decay_rows = jnp.exp(_dot3_r_stacked(et_ref[...], a_last_rows))

    lower = _iota2((chunk, LANES), 1) < SSD_HEAD_DIM
    y_groups = []
    for g in range(SSD_GROUPS):
        gcols = slice(g * SSD_GROUP_WIDTH, (g + 1) * SSD_GROUP_WIDTH)
        cg = c_all[:, g * SSD_STATE:(g + 1) * SSD_STATE]
        bg = b_all[:, g * SSD_STATE:(g + 1) * SSD_STATE]
        cb = _dot_nt(cg, bg)
        h_old = h_ref[gcols, :]
        y_off = _dot_nt(cg, h_old.astype(BF16)) * ea_e[:, gcols]
        pairs = []
        for pr in range(SSD_HPG // 2):
            pcols = slice(g * SSD_GROUP_WIDTH + pr * LANES, g * SSD_GROUP_WIDTH + (pr + 1) * LANES)
            halves = []
            for hh in range(2):
                head = g * SSD_HPG + 2 * pr + hh
                diff = a_cum[:, head:head + 1] - a_cum_t[head:head + 1, :]
                lmat = jnp.where(tril, jnp.exp(diff), 0.0)
                halves.append(_dot((cb * lmat).astype(BF16), xdt_b[:, pcols]))
            pairs.append(jnp.where(lower, halves[0], halves[1]))
        y_groups.append(jnp.concatenate(pairs, axis=1) + y_off)
        h_ref[gcols, :] = decay_rows[gcols, :] * h_old + _dot_tn(x_state[:, gcols], bg)

    y = jnp.concatenate(y_groups, axis=1) + dexp_ref[...] * xs
    gated = y * _silu(z_ref[0])
    normed = []
    for g in range(SSD_GROUPS):
        gg = gated[:, g * SSD_GROUP_WIDTH:(g + 1) * SSD_GROUP_WIDTH]
        normed.append(gg * lax.rsqrt(jnp.mean(gg * gg, axis=-1, keepdims=True) + RMS_EPS))
    g_ref[0] = (jnp.concatenate(normed, axis=1) * nw_ref[...]).astype(g_ref.dtype)

    if n_chunks > 1:
        @pl.when(c < n_chunks - 1)
        def _():
            ext_ref[tail0:CONV_PAD, :] = ext_ref[tail0 + chunk:CONV_PAD + chunk, :]

    @pl.when(c == n_chunks - 1)
    def _():
        hout_ref[n_prev, 0] = h_ref[...]
        for layer in range(n_prev):
            hout_ref[layer, 0] = prev_ref[layer, 0]
        cout_ref[0] = ext_ref[tail0 + last_valid:CONV_PAD + last_valid, :]


def ssd_scan(xbc, z, dt, h0, h0_layer, ctx, conv_w, conv_b, a_log, d_exp, norm_w, expand3, expand_t3,
             chunk, prev_states=None):
    b, t, _ = xbc.shape
    n_chunks = pl.cdiv(t, chunk)
    n_prev = 0 if prev_states is None else prev_states.shape[0]
    tok = lambda w: pl.BlockSpec((1, chunk, w), lambda bi, c: (bi, c, 0))
    per_seq = lambda shape: pl.BlockSpec((1,) + shape, lambda bi, c: (bi, 0, 0))
    states = lambda n, first: pl.BlockSpec((n, 1, SSD_D_INNER, SSD_STATE), lambda bi, c: (first, bi, 0, 0))
    prev = [prev_states] if n_prev else []
    return pl.pallas_call(
        functools.partial(_ssd_scan_kernel, seq_len=t, chunk=chunk, n_prev=n_prev),
        grid=(b, n_chunks),
        in_specs=[tok(SSD_CONV_DIM), tok(SSD_D_INNER), tok(SSD_HEADS),
                  states(1, h0_layer), per_seq((SSD_CONV - 1, SSD_CONV_DIM)),
                  _const_spec((SSD_CONV, SSD_CONV_DIM)), _const_spec((1, SSD_CONV_DIM)),
                  _const_spec((1, SSD_HEADS)), _const_spec((1, SSD_D_INNER)),
                  _const_spec((1, SSD_D_INNER)), _const_spec((3 * SSD_HEADS, SSD_D_INNER)),
                  _const_spec((SSD_D_INNER, 3 * SSD_HEADS))] + [states(n_prev, 0)] * len(prev),
        out_specs=[tok(SSD_D_INNER), states(n_prev + 1, 0), per_seq((SSD_CONV - 1, SSD_CONV_DIM))],
        out_shape=[jax.ShapeDtypeStruct((b, t, SSD_D_INNER), BF16),
                   jax.ShapeDtypeStruct((n_prev + 1, b, SSD_D_INNER, SSD_STATE), F32),
                   jax.ShapeDtypeStruct((b, SSD_CONV - 1, SSD_CONV_DIM), F32)],
        scratch_shapes=[pltpu.VMEM((SSD_D_INNER, SSD_STATE), F32),
                        pltpu.VMEM((CONV_PAD + chunk, SSD_CONV_DIM), F32)],
        compiler_params=_cparams("parallel", "arbitrary"),
        name="ssd_scan",
    )(xbc, z, dt, h0, ctx, conv_w, conv_b, a_log, d_exp, norm_w, expand3, expand_t3, *prev)


def kernel(x_prompt, x_sample, cache_k, cache_v, cache_lf, state_ssm, state_conv, page_table,
           meta_tokens, norm_mix, norm_ffn, norm_final, fox_w_in, fox_b_f, fox_w_out,
           ssd_w_in, ssd_conv_w, ssd_conv_b, ssd_dt_bias, ssd_a_log, ssd_d, ssd_norm, ssd_w_out,
           ffn_w_gate, ffn_w_up, ffn_w_down):
    bp, seq, d = x_prompt.shape
    bs, ls, _ = x_sample.shape
    depth = norm_mix.shape[0]
    t = seq + N_META
    n_pages = page_table.shape[1]
    pt_flat = page_table.reshape(-1).astype(jnp.int32)
    ckt = jnp.transpose(cache_k, (0, 1, 3, 4, 2))
    cvt = jnp.transpose(cache_v, (0, 1, 3, 4, 2))
    clft = jnp.transpose(cache_lf, (0, 1, 3, 2))

    meta = jnp.broadcast_to(meta_tokens[None].astype(x_prompt.dtype), (bp, N_META, d))
    hp = jnp.concatenate([meta, x_prompt], axis=1).reshape(bp * t, d)
    hs = x_sample.reshape(bs * ls, d)

    heads = jnp.arange(SSD_D_INNER, dtype=jnp.int32) // SSD_HEAD_DIM
    expand = (heads[None, :] == jnp.arange(SSD_HEADS, dtype=jnp.int32)[:, None]).astype(BF16)
    expand3 = jnp.tile(expand, (3, 1))
    expand_t3 = jnp.tile(expand.T, (1, 3))
    row = lambda v: v.reshape(1, -1)
    sample_chunk = math.gcd(ls, SSD_CHUNK)
    ssm_in = state_ssm.reshape(state_ssm.shape[0], bs, SSD_D_INNER, SSD_STATE)
    ssm_zero = jnp.zeros((1, bp, SSD_D_INNER, SSD_STATE), F32)
    ffn_wg, ffn_wu, ffn_wd = (w.astype(BF16) for w in (ffn_w_gate, ffn_w_up, ffn_w_down))

    outs = {n: [] for n in ("lfp", "ks", "vs", "lfs", "conv_p", "conv_s")}
    kt_all = vt_all = ssm_p = ssm_s = None
    for i in range(depth):
        j = i // 2
        g_mix = row(norm_mix[i])
        if i % 2 == 0:
            w_in = fox_w_in[j].astype(BF16)
            wq, wk, wv = w_in[:, :d], w_in[:, d:2 * d], w_in[:, 2 * d:3 * d]
            wf = w_in[:, 3 * d:]
            w_out = fox_w_out[j].astype(BF16)
            q, kt_all, vt_all, ktb, vtb, lft = fox_in_prompt(
                hp.reshape(bp, t, d), g_mix, wq, wk.T, wv.T, wf.T, fox_b_f[j].reshape(FOX_HEADS, 1),
                kt_all, vt_all)
            o = fox_flash(q, ktb, vtb, lft)
            hp = matmul_res(o.reshape(bp * t, d), w_out, hp)
            outs["lfp"].append(jnp.transpose(lft, (0, 2, 1)))
            q, k, v, lf = fox_in_sample(hs, g_mix, wq, wk, wv, wf, row(fox_b_f[j]))
            lf3 = lf.reshape(bs, ls, FOX_HEADS)
            o = sample_attn(pt_flat, q.reshape(bs, ls, d), jnp.transpose(lf3, (0, 2, 1)),
                            k.reshape(bs, ls, d), v.reshape(bs, ls, d), ckt, cvt, clft, j, n_pages)
            hs = matmul_res(o.reshape(bs * ls, d), w_out, hs)
            outs["ks"].append(k.reshape(bs, ls, FOX_HEADS, FOX_HEAD_DIM))
            outs["vs"].append(v.reshape(bs, ls, FOX_HEADS, FOX_HEAD_DIM))
            outs["lfs"].append(lf3)
        else:
            w_in = ssd_w_in[j].astype(BF16)
            wz = w_in[:, :SSD_D_INNER]
            wx = w_in[:, SSD_D_INNER:SSD_D_INNER + SSD_CONV_DIM]
            wdt = w_in[:, SSD_D_INNER + SSD_CONV_DIM:]
            w_out = ssd_w_out[j].astype(BF16)
            d_exp = row(jnp.repeat(ssd_d[j].astype(F32), SSD_HEAD_DIM))
            shared = (ssd_conv_w[j], row(ssd_conv_b[j]), row(ssd_a_log[j]), d_exp, row(ssd_norm[j]),
                      expand3, expand_t3)
            z, xbc, dt = ssd_in(hp, g_mix, wz, wx, wdt, row(ssd_dt_bias[j]))
            gated, ssm_p, conv = ssd_scan(
                xbc.reshape(bp, t, SSD_CONV_DIM), z.reshape(bp, t, SSD_D_INNER),
                dt.reshape(bp, t, SSD_HEADS), ssm_zero, 0,
                jnp.zeros((bp, SSD_CONV - 1, SSD_CONV_DIM), F32), *shared, SSD_CHUNK, ssm_p)
            hp = matmul_res(gated.reshape(bp * t, SSD_D_INNER), w_out, hp)
            outs["conv_p"].append(conv)
            z, xbc, dt = ssd_in(hs, g_mix, wz, wx, wdt, row(ssd_dt_bias[j]))
            gated, ssm_s, conv = ssd_scan(
                xbc.reshape(bs, ls, SSD_CONV_DIM), z.reshape(bs, ls, SSD_D_INNER),
                dt.reshape(bs, ls, SSD_HEADS), ssm_in, j, state_conv[j], *shared, sample_chunk, ssm_s)
            hs = matmul_res(gated.reshape(bs * ls, SSD_D_INNER), w_out, hs)
            outs["conv_s"].append(conv)
        last = i == depth - 1
        ffn_w = (row(norm_ffn[i]), ffn_wg, ffn_wu, ffn_wd, i, row(norm_final))
        hp = ffn(hp, *ffn_w, last)
        hs = ffn(hs, *ffn_w, last)

    y_prompt = hp.reshape(bp, t, d)[:, N_META:]
    y_sample = hs.reshape(bs, ls, d)
    st = lambda n: jnp.stack(outs[n])
    n_fox = kt_all.shape[0]
    to_heads = lambda a: jnp.transpose(a.reshape(n_fox, bp, FOX_HEADS, FOX_HEAD_DIM, t), (0, 1, 4, 2, 3))
    split_heads = lambda a: a.reshape(a.shape[:2] + (SSD_HEADS, SSD_HEAD_DIM, SSD_STATE))
    return (y_prompt, y_sample, to_heads(kt_all), to_heads(vt_all), st("lfp"), split_heads(ssm_p),
            st("conv_p"), st("ks"), st("vs"), st("lfs"), split_heads(ssm_s), st("conv_s"))
```

```python
import functools
import math

import jax
import jax.numpy as jnp
from jax import lax
from jax.experimental import pallas as pl
from jax.experimental.pallas import tpu as pltpu

F32 = jnp.float32
BF16 = jnp.bfloat16

D_MODEL = 1024
N_META = 16
FOX_HEADS = 16
FOX_HEAD_DIM = 64
PAGE_SIZE = 128
SSD_D_INNER = 2048
SSD_HEAD_DIM = 64
SSD_HEADS = 32
SSD_GROUPS = 4
SSD_HPG = 8
SSD_STATE = 128
SSD_CONV = 4
SSD_CONV_DIM = 3072
SSD_CHUNK = 128
SSD_GROUP_WIDTH = SSD_HPG * SSD_HEAD_DIM
FFN_HIDDEN = 2816
RMS_EPS = 1e-6
NEG_BIG = -1e30
LOG2E = math.log2(math.e)

LANES = 128
VMEM_LIMIT = 56 * 1024 * 1024
FLASH_BLOCK = 256
FOX_IN_TILE = 384
FFN_CHUNK = 256
SAMPLE_MAX_PAGES = 16
SAMPLE_SEQS_PER_STEP = 2


def _cparams(*sem):
    return pltpu.CompilerParams(dimension_semantics=sem, vmem_limit_bytes=VMEM_LIMIT)


def _dot(a, b):
    return jnp.dot(a, b, preferred_element_type=F32)


def _dot_nt(a, b):
    return lax.dot_general(a, b, (((1,), (1,)), ((), ())), preferred_element_type=F32)


def _dot_tn(a, b):
    return lax.dot_general(a, b, (((0,), (0,)), ((), ())), preferred_element_type=F32)


def _split3(x):
    hi = x.astype(BF16)
    r1 = x - hi.astype(F32)
    mid = r1.astype(BF16)
    lo = (r1 - mid.astype(F32)).astype(BF16)
    return hi, mid, lo


def _dot3_r(a_exact, x):
    hi, mid, lo = _split3(x)
    return _dot(a_exact, lo) + _dot(a_exact, mid) + _dot(a_exact, hi)


def _dot3_l(x, b_exact):
    hi, mid, lo = _split3(x)
    return _dot(lo, b_exact) + _dot(mid, b_exact) + _dot(hi, b_exact)


def _dot3_nt_r(a_exact, x):
    hi, mid, lo = _split3(x)
    return _dot_nt(a_exact, lo) + _dot_nt(a_exact, mid) + _dot_nt(a_exact, hi)


def _dot3_l_stacked(x, b_exact_x3):
    return _dot(jnp.concatenate(_split3(x), axis=1), b_exact_x3)


def _dot3_r_stacked(a_exact_x3, x):
    return _dot(a_exact_x3, jnp.concatenate(_split3(x), axis=0))


def _rms(x, g):
    ms = jnp.mean(x * x, axis=-1, keepdims=True)
    return x * lax.rsqrt(ms + RMS_EPS) * g


def _log_sigmoid(x):
    return jnp.minimum(x, 0.0) - jnp.log1p(jnp.exp(-jnp.abs(x)))


def _softplus(x):
    return jnp.maximum(x, 0.0) + jnp.log1p(jnp.exp(-jnp.abs(x)))


def _silu(x):
    return x * (1.0 / (1.0 + jnp.exp(-x)))


def _iota2(shape, dim):
    return lax.broadcasted_iota(jnp.int32, shape, dim)


def _tile(m, candidates):
    for c in candidates:
        if m % c == 0:
            return c
    return m


def _const_spec(shape):
    nd = len(shape)
    return pl.BlockSpec(shape, lambda *_: (0,) * nd)


def _resident_spec(shape, index_map):
    return pl.BlockSpec(shape, index_map, pipeline_mode=pl.Buffered(1))


def _ssd_in_kernel(x_ref, g_ref, wz_ref, wx_ref, wdt_ref, dtb_ref, z_ref, xbc_ref, dt_ref):
    xn = _rms(x_ref[...], g_ref[...]).astype(BF16)
    z_ref[...] = _dot(xn, wz_ref[...])
    xbc_ref[...] = _dot(xn, wx_ref[...])
    dt_ref[...] = _softplus(_dot(xn, wdt_ref[...]) + dtb_ref[...])


def ssd_in(x, g, wz, wx, wdt, dt_bias):
    m = x.shape[0]
    tm = _tile(m, (384, 256, 128, 144, 96, 48, 16))
    row = lambda w: pl.BlockSpec((tm, w), lambda i: (i, 0))
    return pl.pallas_call(
        _ssd_in_kernel,
        grid=(m // tm,),
        in_specs=[row(D_MODEL), _const_spec((1, D_MODEL)),
                  _const_spec((D_MODEL, SSD_D_INNER)), _const_spec((D_MODEL, SSD_CONV_DIM)),
                  _const_spec((D_MODEL, SSD_HEADS)), _const_spec((1, SSD_HEADS))],
        out_specs=[row(SSD_D_INNER), row(SSD_CONV_DIM), row(SSD_HEADS)],
        out_shape=[jax.ShapeDtypeStruct((m, SSD_D_INNER), F32),
                   jax.ShapeDtypeStruct((m, SSD_CONV_DIM), F32),
                   jax.ShapeDtypeStruct((m, SSD_HEADS), F32)],
        compiler_params=_cparams("parallel"),
        name="ssd_in",
    )(x, g, wz, wx, wdt, dt_bias)


def _mix_ffn_kernel(a_ref, wo_ref, r_ref, g_ref, wg_ref, wu_ref, wd_ref, gf_ref, o_ref, acc_ref,
                    *, final_norm):
    x = r_ref[...] + _dot(a_ref[...], wo_ref[...])
    xn = _rms(x, g_ref[...]).astype(BF16)
    acc_ref[...] = x
    for c in range(FFN_HIDDEN // FFN_CHUNK):
        cols = slice(c * FFN_CHUNK, (c + 1) * FFN_CHUNK)
        gate = _dot(xn, wg_ref[0, :, cols])
        up = _dot(xn, wu_ref[0, :, cols])
        act = (_silu(gate) * up).astype(BF16)
        acc_ref[...] += _dot(act, wd_ref[0, cols, :])
    out = acc_ref[...]
    o_ref[...] = _rms(out, gf_ref[...]) if final_norm else out


def mix_ffn(a, w_out, res, g, wg, wu, wd, layer, g_final, final_norm):
    m, k = a.shape
    tm = _tile(m, (688, 512, 256, 128, 144, 96, 48, 16))
    rows = lambda w: pl.BlockSpec((tm, w), lambda i: (i, 0))
    weight = lambda r, c: _resident_spec((1, r, c), lambda i: (layer, 0, 0))
    return pl.pallas_call(
        functools.partial(_mix_ffn_kernel, final_norm=final_norm),
        grid=(m // tm,),
        in_specs=[rows(k), _resident_spec((k, D_MODEL), lambda i: (0, 0)), rows(D_MODEL),
                  _const_spec((1, D_MODEL)), weight(D_MODEL, FFN_HIDDEN), weight(D_MODEL, FFN_HIDDEN),
                  weight(FFN_HIDDEN, D_MODEL), _const_spec((1, D_MODEL))],
        out_specs=rows(D_MODEL),
        out_shape=jax.ShapeDtypeStruct((m, D_MODEL), F32),
        scratch_shapes=[pltpu.VMEM((tm, D_MODEL), F32)],
        compiler_params=_cparams("parallel"),
        name="mix_ffn",
    )(a, w_out, res, g, wg, wu, wd, g_final)


def _fox_in_prompt_kernel(x_ref, g_ref, wq_ref, wkt_ref, wvt_ref, wft_ref, bft_ref, *refs, n_prev):
    prev_k_ref, prev_v_ref = refs[:2] if n_prev else (None, None)
    q_ref, kt_ref, vt_ref, ktb_ref, vtb_ref, lft_ref = refs[2 if n_prev else 0:]
    xn = _rms(x_ref[0], g_ref[...]).astype(BF16)
    q_ref[0] = (_dot(xn, wq_ref[...]) * (FOX_HEAD_DIM ** -0.5 * LOG2E)).astype(BF16)
    kt = _dot_nt(wkt_ref[...], xn)
    kt_ref[n_prev, 0] = kt
    ktb_ref[0] = kt.astype(BF16)
    vt = _dot_nt(wvt_ref[...], xn)
    vt_ref[n_prev, 0] = vt
    vtb_ref[0] = vt.astype(BF16)
    lft_ref[0] = _log_sigmoid(_dot_nt(wft_ref[...], xn) + bft_ref[...])
    for layer in range(n_prev):
        kt_ref[layer, 0] = prev_k_ref[layer, 0]
        vt_ref[layer, 0] = prev_v_ref[layer, 0]


def fox_in_prompt(x, g, wq, wkt, wvt, wft, b_ft, prev_kt=None, prev_vt=None):
    b, t, d = x.shape
    n_prev = 0 if prev_kt is None else prev_kt.shape[0]
    tt = FOX_IN_TILE if t > FOX_IN_TILE else t
    rows = pl.BlockSpec((1, tt, d), lambda bi, i: (bi, i, 0))
    cols = lambda r: pl.BlockSpec((1, r, tt), lambda bi, i: (bi, 0, i))
    stack = lambda n: pl.BlockSpec((n, 1, d, tt), lambda bi, i: (0, bi, 0, i))
    prev = [prev_kt, prev_vt] if n_prev else []
    return pl.pallas_call(
        functools.partial(_fox_in_prompt_kernel, n_prev=n_prev),
        grid=(b, pl.cdiv(t, tt)),
        in_specs=[rows, _const_spec((1, d)), _const_spec((d, d)), _const_spec((d, d)),
                  _const_spec((d, d)), _const_spec((FOX_HEADS, d)), _const_spec((FOX_HEADS, 1))]
        + [stack(n_prev)] * len(prev),
        out_specs=[rows, stack(n_prev + 1), stack(n_prev + 1), cols(d), cols(d), cols(FOX_HEADS)],
        out_shape=[jax.ShapeDtypeStruct((b, t, d), BF16),
                   jax.ShapeDtypeStruct((n_prev + 1, b, d, t), F32),
                   jax.ShapeDtypeStruct((n_prev + 1, b, d, t), F32),
                   jax.ShapeDtypeStruct((b, d, t), BF16), jax.ShapeDtypeStruct((b, d, t), BF16),
                   jax.ShapeDtypeStruct((b, FOX_HEADS, t), F32)],
        compiler_params=_cparams("parallel", "parallel"),
        name="fox_in_prompt",
    )(x, g, wq, wkt, wvt, wft, b_ft, *prev)


def _fox_in_sample_kernel(x_ref, g_ref, wq_ref, wk_ref, wv_ref, wf_ref, bf_ref,
                          q_ref, k_ref, v_ref, lf_ref):
    xn = _rms(x_ref[...], g_ref[...]).astype(BF16)
    q_ref[...] = _dot(xn, wq_ref[...]) * (FOX_HEAD_DIM ** -0.5)
    k_ref[...] = _dot(xn, wk_ref[...])
    v_ref[...] = _dot(xn, wv_ref[...])
    lf_ref[...] = _log_sigmoid(_dot(xn, wf_ref[...]) + bf_ref[...])


def fox_in_sample(x, g, wq, wk, wv, wf, b_f):
    m, d = x.shape
    tm = _tile(m, (256, 128, 64, 16))
    row = lambda w: pl.BlockSpec((tm, w), lambda i: (i, 0))
    return pl.pallas_call(
        _fox_in_sample_kernel,
        grid=(m // tm,),
        in_specs=[row(d), _const_spec((1, d)), _const_spec((d, d)), _const_spec((d, d)),
                  _const_spec((d, d)), _const_spec((d, FOX_HEADS)), _const_spec((1, FOX_HEADS))],
        out_specs=[row(d), row(d), row(d), row(FOX_HEADS)],
        out_shape=[jax.ShapeDtypeStruct((m, d), F32)] * 3 + [jax.ShapeDtypeStruct((m, FOX_HEADS), F32)],
        compiler_params=_cparams("parallel"),
        name="fox_in_sample",
    )(x, g, wq, wk, wv, wf, b_f)


def _flash_kernel(q_ref, kt_ref, vt_ref, lft_ref, o_ref,
                  ft_ref, f_ref, qm_ref, fqb_ref, m_ref, l_ref, acc_ref, *, seq_len):
    blk = FLASH_BLOCK
    n_full = seq_len // blk
    tail = seq_len % blk
    qi = pl.program_id(1)

    @pl.when(qi == 0)
    def _():
        carry = jnp.zeros((FOX_HEADS, 1), F32)
        for start in range(0, seq_len, LANES):
            w = min(LANES, seq_len - start)
            triu = (_iota2((w, w), 0) <= _iota2((w, w), 1)).astype(BF16)
            eye = (_iota2((w, w), 0) == _iota2((w, w), 1)).astype(BF16)
            ft = carry + _dot3_l(lft_ref[0, :, start:start + w], triu)
            carry = ft[:, w - 1:w]
            ft = ft * LOG2E
            ft_ref[:, start:start + w] = ft
            f_ref[start:start + w, :] = _dot3_nt_r(eye, ft)

    def attend(rows, row0, n_before, diag_start, diag_len):
        lower = _iota2((rows, LANES), 1) < FOX_HEAD_DIM
        fq = f_ref[pl.ds(row0, rows), :]
        for pair in range(FOX_HEADS // 2):
            qp = q_ref[0, 0:rows, pair * LANES:(pair + 1) * LANES]
            zero = jnp.zeros_like(qp)
            qm_ref[2 * pair, 0:rows, :] = jnp.where(lower, qp, zero)
            qm_ref[2 * pair + 1, 0:rows, :] = jnp.where(lower, zero, qp)
        for h in range(FOX_HEADS):
            fqb_ref[h, 0:rows, :] = jnp.broadcast_to(fq[:, h:h + 1], (rows, LANES))
        m_ref[:, 0:rows, :] = jnp.full((FOX_HEADS, rows, LANES), NEG_BIG, F32)
        l_ref[:, 0:rows, :] = jnp.zeros((FOX_HEADS, rows, LANES), F32)
        acc_ref[:, 0:rows, :] = jnp.zeros((FOX_HEADS, rows, LANES), F32)

        def lanes_like(x, n):
            return x[:, 0:n] if n <= LANES else jnp.concatenate([x] * (n // LANES), axis=1)

        def chunk(kstart, klen, mask):
            for pair in range(FOX_HEADS // 2):
                prow = slice(pair * LANES, (pair + 1) * LANES)
                kc = kt_ref[0, prow, pl.ds(kstart, klen)]
                vc = vt_ref[0, prow, pl.ds(kstart, klen)]
                for h in (2 * pair, 2 * pair + 1):
                    fk = ft_ref[h:h + 1, pl.ds(kstart, klen)]
                    s = _dot(qm_ref[h, 0:rows, :], kc) + (lanes_like(fqb_ref[h, 0:rows, :], klen) - fk)
                    if mask is not None:
                        s = jnp.where(mask, s, NEG_BIG)
                    m_old = m_ref[h, 0:rows, :]
                    m_new = jnp.maximum(m_old, jnp.max(s, axis=-1, keepdims=True))
                    alpha = jnp.exp2(m_old - m_new)
                    p = jnp.exp2(s - lanes_like(m_new, klen))
                    l_ref[h, 0:rows, :] = alpha * l_ref[h, 0:rows, :] + jnp.sum(p, axis=-1, keepdims=True)
                    m_ref[h, 0:rows, :] = m_new
                    acc_ref[h, 0:rows, :] = acc_ref[h, 0:rows, :] * alpha + _dot_nt(p.astype(BF16), vc)

        if isinstance(n_before, int):
            per_pass = max(1, (32 * 8 * LANES) // (rows * blk))
            for c0 in range(0, n_before, per_pass):
                chunk(c0 * blk, min(per_pass, n_before - c0) * blk, None)
        else:
            def body(c, carry):
                chunk(pl.multiple_of(c * blk, blk), blk, None)
                return carry

            lax.fori_loop(0, n_before, body, 0)
        causal = _iota2((rows, diag_len), 1) <= _iota2((rows, diag_len), 0)
        chunk(diag_start, diag_len, causal)
        for pair in range(FOX_HEADS // 2):
            o0 = acc_ref[2 * pair, 0:rows, :] * (1.0 / l_ref[2 * pair, 0:rows, :])
            o1 = acc_ref[2 * pair + 1, 0:rows, :] * (1.0 / l_ref[2 * pair + 1, 0:rows, :])
            o_ref[0, 0:rows, pair * LANES:(pair + 1) * LANES] = jnp.where(lower, o0, o1).astype(o_ref.dtype)

    if n_full:
        @pl.when(qi < n_full)
        def _():
            start = pl.multiple_of(qi * blk, blk)
            attend(blk, start, qi, start, blk)

    if tail:
        @pl.when(qi == n_full)
        def _():
            attend(tail, n_full * blk, n_full, n_full * blk, tail)


def fox_flash(q, ktb, vtb, lft):
    b, t, d = q.shape
    nq = pl.cdiv(t, FLASH_BLOCK)
    t_pad = nq * FLASH_BLOCK
    whole = lambda r: pl.BlockSpec((1, r, t), lambda bi, i: (bi, 0, 0))
    head_rows = (FOX_HEADS, FLASH_BLOCK, LANES)
    return pl.pallas_call(
        functools.partial(_flash_kernel, seq_len=t),
        grid=(b, nq),
        in_specs=[pl.BlockSpec((1, FLASH_BLOCK, d), lambda bi, i: (bi, i, 0)),
                  whole(d), whole(d), whole(FOX_HEADS)],
        out_specs=pl.BlockSpec((1, FLASH_BLOCK, d), lambda bi, i: (bi, i, 0)),
        out_shape=jax.ShapeDtypeStruct((b, t, d), BF16),
        scratch_shapes=[pltpu.VMEM((FOX_HEADS, t_pad), F32), pltpu.VMEM((t_pad, FOX_HEADS), F32),
                        pltpu.VMEM(head_rows, BF16), pltpu.VMEM(head_rows, F32),
                        pltpu.VMEM(head_rows, F32), pltpu.VMEM(head_rows, F32),
                        pltpu.VMEM(head_rows, F32)],
        compiler_params=_cparams("parallel", "arbitrary"),
        name="fox_flash",
    )(q, ktb, vtb, lft)


def _sample_attn_kernel(pt_ref, q_ref, lftn_ref, kn_ref, vn_ref, *refs, n_pages, n_new):
    ck_refs, cv_refs, clf_refs = refs[0:n_pages], refs[n_pages:2 * n_pages], refs[2 * n_pages:3 * n_pages]
    o_ref = refs[3 * n_pages]
    rows = n_new * FOX_HEADS
    diag = (_iota2((rows, D_MODEL), 0) % FOX_HEADS) == (_iota2((rows, D_MODEL), 1) // FOX_HEAD_DIM)

    q = q_ref[0]
    qrep = jnp.concatenate(
        [jnp.broadcast_to(q[i:i + 1, :], (FOX_HEADS, D_MODEL)) for i in range(n_new)], axis=0)
    qbd = jnp.where(diag, qrep, 0.0).astype(BF16)
    triu = (_iota2((n_new, n_new), 0) <= _iota2((n_new, n_new), 1)).astype(BF16)
    nqt = _dot3_l(lftn_ref[0], triu)
    fqb = jnp.concatenate(
        [jnp.broadcast_to(nqt[:, i:i + 1], (FOX_HEADS, LANES)) for i in range(n_new)], axis=0)

    after = (_iota2((PAGE_SIZE, PAGE_SIZE), 0) > _iota2((PAGE_SIZE, PAGE_SIZE), 1)).astype(BF16)
    ones = jnp.ones((PAGE_SIZE, PAGE_SIZE), BF16)
    carry = jnp.zeros((FOX_HEADS, PAGE_SIZE), F32)
    bias = []
    for i in range(n_pages):
        lft = clf_refs[i][0, 0]
        later = carry + _dot3_l(lft, after)
        carry = carry + _dot3_l(lft, ones)
        bias.append(fqb + jnp.concatenate([later] * n_new, axis=0))
    kt = jnp.concatenate(
        [ck_refs[i][0, 0].reshape(D_MODEL, PAGE_SIZE).astype(BF16) for i in range(n_pages)], axis=1)
    s_past = _dot(qbd, kt) + jnp.concatenate(bias, axis=1)
    s_new = _dot_nt(qbd, kn_ref[0].astype(BF16)) + (
        fqb[:, 0:n_new] - jnp.concatenate([nqt] * n_new, axis=0))
    causal = _iota2((rows, n_new), 1) <= _iota2((rows, n_new), 0) // FOX_HEADS
    s_new = jnp.where(causal, s_new, NEG_BIG)

    m = jnp.maximum(jnp.max(s_past, axis=-1, keepdims=True), jnp.max(s_new, axis=-1, keepdims=True))
    p_past = jnp.exp(s_past - m)
    p_new = jnp.exp(s_new - m)
    denom = jnp.sum(p_past, axis=-1, keepdims=True) + jnp.sum(p_new, axis=-1, keepdims=True)
    vt = jnp.concatenate(
        [cv_refs[i][0, 0].reshape(D_MODEL, PAGE_SIZE).astype(BF16) for i in range(n_pages)], axis=1)
    acc = _dot_nt(p_past.astype(BF16), vt) + _dot(p_new.astype(BF16), vn_ref[0].astype(BF16))
    out = jnp.where(diag, acc * (1.0 / denom), 0.0).astype(BF16)
    pick = (_iota2((n_new, rows), 1) // FOX_HEADS == _iota2((n_new, rows), 0)).astype(BF16)
    o_ref[0] = _dot(pick, out).astype(o_ref.dtype)


def sample_attn(page_table_flat, q, lft_new, k_new, v_new, ckt, cvt, clft, layer, n_pages):
    b, n_new, d = q.shape
    assert n_pages <= SAMPLE_MAX_PAGES, "all pages of a sequence must fit one grid step's VMEM"
    seq = lambda shape: pl.BlockSpec((1,) + shape, lambda bi, pt: (bi, 0, 0))

    def page(i, tail_shape):
        def index_map(bi, pt):
            return (layer, pt[bi * n_pages + (n_pages - 1 - i)]) + (0,) * len(tail_shape)
        return pl.BlockSpec((1, 1) + tail_shape, index_map)

    kv_shape = (FOX_HEADS, FOX_HEAD_DIM, PAGE_SIZE)
    grid_spec = pltpu.PrefetchScalarGridSpec(
        num_scalar_prefetch=1,
        grid=(b,),
        in_specs=[seq((n_new, d)), seq((FOX_HEADS, n_new)), seq((n_new, d)), seq((n_new, d))]
        + [page(i, kv_shape) for i in range(n_pages)] + [page(i, kv_shape) for i in range(n_pages)]
        + [page(i, (FOX_HEADS, PAGE_SIZE)) for i in range(n_pages)],
        out_specs=seq((n_new, d)),
    )
    return pl.pallas_call(
        functools.partial(_sample_attn_kernel, n_pages=n_pages, n_new=n_new),
        grid_spec=grid_spec,
        out_shape=jax.ShapeDtypeStruct((b, n_new, d), BF16),
        compiler_params=_cparams("parallel"),
        name="sample_attn",
    )(page_table_flat, q, lft_new, k_new, v_new,
      *([ckt] * n_pages), *([cvt] * n_pages), *([clft] * n_pages))


CONV_PAD = 8


def _ssd_scan_kernel(xbc_ref, z_ref, dt_ref, h0_ref, ctx_ref, cw_ref, cb_ref, alog_ref, dexp_ref,
                     nw_ref, e_ref, et_ref, *refs, seq_len, chunk, n_prev, seqs):
    prev_ref = refs[0] if n_prev else None
    g_ref, hout_ref, cout_ref, h_all_ref, ext_all_ref = refs[1 if n_prev else 0:]
    for s in range(seqs):
        _ssd_scan_one(s, xbc_ref, z_ref, dt_ref, h0_ref, ctx_ref, cw_ref, cb_ref, alog_ref, dexp_ref,
                      nw_ref, e_ref, et_ref, prev_ref, g_ref, hout_ref, cout_ref,
                      h_all_ref.at[s], ext_all_ref.at[s], seq_len=seq_len, chunk=chunk, n_prev=n_prev)


def _ssd_scan_one(s, xbc_ref, z_ref, dt_ref, h0_ref, ctx_ref, cw_ref, cb_ref, alog_ref, dexp_ref,
                  nw_ref, e_ref, et_ref, prev_ref, g_ref, hout_ref, cout_ref, h_ref, ext_ref,
                  *, seq_len, chunk, n_prev):
    c = pl.program_id(1)
    n_chunks = pl.cdiv(seq_len, chunk)
    last_valid = seq_len - (n_chunks - 1) * chunk
    tail0 = CONV_PAD - (SSD_CONV - 1)

    @pl.when(c == 0)
    def _():
        h_ref[...] = h0_ref[0, s]
        ext_ref[tail0:CONV_PAD, :] = ctx_ref[s]

    ext_ref[CONV_PAD:CONV_PAD + chunk, :] = xbc_ref[s]
    cw = cw_ref[...]
    xc = cb_ref[...]
    for w in range(SSD_CONV):
        xc = xc + cw[w:w + 1, :] * ext_ref[tail0 + w:tail0 + w + chunk, :]
    xc = _silu(xc)
    dt = dt_ref[s]
    if last_valid != chunk:
        n_valid = jnp.where(c == n_chunks - 1, last_valid, chunk)
        xc = jnp.where(_iota2(xc.shape, 0) < n_valid, xc, 0.0)
        dt = jnp.where(_iota2(dt.shape, 0) < n_valid, dt, 0.0)
    xs = xc[:, :SSD_D_INNER]
    b_all = xc[:, SSD_D_INNER:SSD_D_INNER + SSD_GROUPS * SSD_STATE].astype(BF16)
    c_all = xc[:, SSD_D_INNER + SSD_GROUPS * SSD_STATE:].astype(BF16)

    a_neg = -jnp.exp(alog_ref[...])
    tril = (_iota2((chunk, chunk), 0) >= _iota2((chunk, chunk), 1))
    a_cum = _dot3_r(tril.astype(BF16), dt * a_neg)
    eye = (_iota2((SSD_HEADS, SSD_HEADS), 0) == _iota2((SSD_HEADS, SSD_HEADS), 1)).astype(BF16)
    a_cum_t = _dot3_nt_r(eye, a_cum)
    expand3 = e_ref[...]
    dt_e = _dot3_l_stacked(dt, expand3)
    ac_e = _dot3_l_stacked(a_cum, expand3)
    al_e = ac_e[chunk - 1:chunk, :]
    xdt = xs * dt_e
    xdt_b = xdt.astype(BF16)
    x_state = (xdt * jnp.exp(al_e - ac_e)).astype(BF16)
    ea_e = jnp.exp(ac_e)
    a_last_rows = jnp.broadcast_to(a_cum_t[:, chunk - 1:chunk], (SSD_HEADS, SSD_STATE))
    decay_rows = jnp.exp(_dot3_r_stacked(et_ref[...], a_last_rows))

    lower = _iota2((chunk, LANES), 1) < SSD_HEAD_DIM
    y_groups = []
    for g in range(SSD_GROUPS):
        gcols = slice(g * SSD_GROUP_WIDTH, (g + 1) * SSD_GROUP_WIDTH)
        cg = c_all[:, g * SSD_STATE:(g + 1) * SSD_STATE]
        bg = b_all[:, g * SSD_STATE:(g + 1) * SSD_STATE]
        cb = _dot_nt(cg, bg)
        h_old = h_ref[gcols, :]
        y_off = _dot_nt(cg, h_old.astype(BF16)) * ea_e[:, gcols]
        pairs = []
        for pr in range(SSD_HPG // 2):
            pcols = slice(g * SSD_GROUP_WIDTH + pr * LANES, g * SSD_GROUP_WIDTH + (pr + 1) * LANES)
            halves = []
            for hh in range(2):
                head = g * SSD_HPG + 2 * pr + hh
                diff = a_cum[:, head:head + 1] - a_cum_t[head:head + 1, :]
                lmat = jnp.where(tril, jnp.exp(diff), 0.0)
                halves.append(_dot((cb * lmat).astype(BF16), xdt_b[:, pcols]))
            pairs.append(jnp.where(lower, halves[0], halves[1]))
        y_groups.append(jnp.concatenate(pairs, axis=1) + y_off)
        h_ref[gcols, :] = decay_rows[gcols, :] * h_old + _dot_tn(x_state[:, gcols], bg)

    y = jnp.concatenate(y_groups, axis=1) + dexp_ref[...] * xs
    gated = y * _silu(z_ref[s])
    normed = []
    for g in range(SSD_GROUPS):
        gg = gated[:, g * SSD_GROUP_WIDTH:(g + 1) * SSD_GROUP_WIDTH]
        normed.append(gg * lax.rsqrt(jnp.mean(gg * gg, axis=-1, keepdims=True) + RMS_EPS))
    g_ref[s] = (jnp.concatenate(normed, axis=1) * nw_ref[...]).astype(g_ref.dtype)

    if n_chunks > 1:
        @pl.when(c < n_chunks - 1)
        def _():
            ext_ref[tail0:CONV_PAD, :] = ext_ref[tail0 + chunk:CONV_PAD + chunk, :]

    @pl.when(c == n_chunks - 1)
    def _():
        hout_ref[n_prev, s] = h_ref[...]
        for layer in range(n_prev):
            hout_ref[layer, s] = prev_ref[layer, s]
        cout_ref[s] = ext_ref[tail0 + last_valid:CONV_PAD + last_valid, :]


def ssd_scan(xbc, z, dt, h0, h0_layer, ctx, conv_w, conv_b, a_log, d_exp, norm_w, expand3, expand_t3,
             chunk, prev_states=None, seqs=1):
    b, t, _ = xbc.shape
    n_chunks = pl.cdiv(t, chunk)
    n_prev = 0 if prev_states is None else prev_states.shape[0]
    tok = lambda w: pl.BlockSpec((seqs, chunk, w), lambda bi, c: (bi, c, 0))
    per_seq = lambda shape: pl.BlockSpec((seqs,) + shape, lambda bi, c: (bi, 0, 0))
    states = lambda n, first: pl.BlockSpec((n, seqs, SSD_D_INNER, SSD_STATE), lambda bi, c: (first, bi, 0, 0))
    prev = [prev_states] if n_prev else []
    return pl.pallas_call(
        functools.partial(_ssd_scan_kernel, seq_len=t, chunk=chunk, n_prev=n_prev, seqs=seqs),
        grid=(b // seqs, n_chunks),
        in_specs=[tok(SSD_CONV_DIM), tok(SSD_D_INNER), tok(SSD_HEADS),
                  states(1, h0_layer), per_seq((SSD_CONV - 1, SSD_CONV_DIM)),
                  _const_spec((SSD_CONV, SSD_CONV_DIM)), _const_spec((1, SSD_CONV_DIM)),
                  _const_spec((1, SSD_HEADS)), _const_spec((1, SSD_D_INNER)),
                  _const_spec((1, SSD_D_INNER)), _const_spec((3 * SSD_HEADS, SSD_D_INNER)),
                  _const_spec((SSD_D_INNER, 3 * SSD_HEADS))] + [states(n_prev, 0)] * len(prev),
        out_specs=[tok(SSD_D_INNER), states(n_prev + 1, 0), per_seq((SSD_CONV - 1, SSD_CONV_DIM))],
        out_shape=[jax.ShapeDtypeStruct((b, t, SSD_D_INNER), BF16),
                   jax.ShapeDtypeStruct((n_prev + 1, b, SSD_D_INNER, SSD_STATE), F32),
                   jax.ShapeDtypeStruct((b, SSD_CONV - 1, SSD_CONV_DIM), F32)],
        scratch_shapes=[pltpu.VMEM((seqs, SSD_D_INNER, SSD_STATE), F32),
                        pltpu.VMEM((seqs, CONV_PAD + chunk, SSD_CONV_DIM), F32)],
        compiler_params=_cparams("parallel", "arbitrary"),
        name="ssd_scan",
    )(xbc, z, dt, h0, ctx, conv_w, conv_b, a_log, d_exp, norm_w, expand3, expand_t3, *prev)


def kernel(x_prompt, x_sample, cache_k, cache_v, cache_lf, state_ssm, state_conv, page_table,
           meta_tokens, norm_mix, norm_ffn, norm_final, fox_w_in, fox_b_f, fox_w_out,
           ssd_w_in, ssd_conv_w, ssd_conv_b, ssd_dt_bias, ssd_a_log, ssd_d, ssd_norm, ssd_w_out,
           ffn_w_gate, ffn_w_up, ffn_w_down):
    bp, seq, d = x_prompt.shape
    bs, ls, _ = x_sample.shape
    depth = norm_mix.shape[0]
    t = seq + N_META
    n_pages = page_table.shape[1]
    pt_flat = page_table.reshape(-1).astype(jnp.int32)
    ckt = jnp.transpose(cache_k, (0, 1, 3, 4, 2))
    cvt = jnp.transpose(cache_v, (0, 1, 3, 4, 2))
    clft = jnp.transpose(cache_lf, (0, 1, 3, 2))

    meta = jnp.broadcast_to(meta_tokens[None].astype(x_prompt.dtype), (bp, N_META, d))
    hp = jnp.concatenate([meta, x_prompt], axis=1).reshape(bp * t, d)
    hs = x_sample.reshape(bs * ls, d)

    heads = jnp.arange(SSD_D_INNER, dtype=jnp.int32) // SSD_HEAD_DIM
    expand = (heads[None, :] == jnp.arange(SSD_HEADS, dtype=jnp.int32)[:, None]).astype(BF16)
    expand3 = jnp.tile(expand, (3, 1))
    expand_t3 = jnp.tile(expand.T, (1, 3))
    row = lambda v: v.reshape(1, -1)
    sample_chunk = math.gcd(ls, SSD_CHUNK)
    ssm_in = state_ssm.reshape(state_ssm.shape[0], bs, SSD_D_INNER, SSD_STATE)
    ssm_zero = jnp.zeros((1, bp, SSD_D_INNER, SSD_STATE), F32)
    ffn_wg, ffn_wu, ffn_wd = (w.astype(BF16) for w in (ffn_w_gate, ffn_w_up, ffn_w_down))

    outs = {n: [] for n in ("lfp", "ks", "vs", "lfs", "conv_p", "conv_s")}
    kt_all = vt_all = ssm_p = ssm_s = None
    for i in range(depth):
        j = i // 2
        g_mix = row(norm_mix[i])
        if i % 2 == 0:
            w_in = fox_w_in[j].astype(BF16)
            wq, wk, wv = w_in[:, :d], w_in[:, d:2 * d], w_in[:, 2 * d:3 * d]
            wf = w_in[:, 3 * d:]
            w_out = fox_w_out[j].astype(BF16)
            q, kt_all, vt_all, ktb, vtb, lft = fox_in_prompt(
                hp.reshape(bp, t, d), g_mix, wq, wk.T, wv.T, wf.T, fox_b_f[j].reshape(FOX_HEADS, 1),
                kt_all, vt_all)
            mix_p = fox_flash(q, ktb, vtb, lft).reshape(bp * t, d)
            outs["lfp"].append(jnp.transpose(lft, (0, 2, 1)))
            q, k, v, lf = fox_in_sample(hs, g_mix, wq, wk, wv, wf, row(fox_b_f[j]))
            lf3 = lf.reshape(bs, ls, FOX_HEADS)
            mix_s = sample_attn(pt_flat, q.reshape(bs, ls, d), jnp.transpose(lf3, (0, 2, 1)),
                                k.reshape(bs, ls, d), v.reshape(bs, ls, d), ckt, cvt, clft, j,
                                n_pages).reshape(bs * ls, d)
            outs["ks"].append(k.reshape(bs, ls, FOX_HEADS, FOX_HEAD_DIM))
            outs["vs"].append(v.reshape(bs, ls, FOX_HEADS, FOX_HEAD_DIM))
            outs["lfs"].append(lf3)
        else:
            w_in = ssd_w_in[j].astype(BF16)
            wz = w_in[:, :SSD_D_INNER]
            wx = w_in[:, SSD_D_INNER:SSD_D_INNER + SSD_CONV_DIM]
            wdt = w_in[:, SSD_D_INNER + SSD_CONV_DIM:]
            w_out = ssd_w_out[j].astype(BF16)
            d_exp = row(jnp.repeat(ssd_d[j].astype(F32), SSD_HEAD_DIM))
            shared = (ssd_conv_w[j], row(ssd_conv_b[j]), row(ssd_a_log[j]), d_exp, row(ssd_norm[j]),
                      expand3, expand_t3)
            z, xbc, dt = ssd_in(hp, g_mix, wz, wx, wdt, row(ssd_dt_bias[j]))
            gated, ssm_p, conv = ssd_scan(
                xbc.reshape(bp, t, SSD_CONV_DIM), z.reshape(bp, t, SSD_D_INNER),
                dt.reshape(bp, t, SSD_HEADS), ssm_zero, 0,
                jnp.zeros((bp, SSD_CONV - 1, SSD_CONV_DIM), F32), *shared, SSD_CHUNK, ssm_p)
            mix_p = gated.reshape(bp * t, SSD_D_INNER)
            outs["conv_p"].append(conv)
            z, xbc, dt = ssd_in(hs, g_mix, wz, wx, wdt, row(ssd_dt_bias[j]))
            gated, ssm_s, conv = ssd_scan(
                xbc.reshape(bs, ls, SSD_CONV_DIM), z.reshape(bs, ls, SSD_D_INNER),
                dt.reshape(bs, ls, SSD_HEADS), ssm_in, j, state_conv[j], *shared, sample_chunk, ssm_s,
                seqs=_tile(bs, (SAMPLE_SEQS_PER_STEP, 1)))
            mix_s = gated.reshape(bs * ls, SSD_D_INNER)
            outs["conv_s"].append(conv)
        last = i == depth - 1
        ffn_w = (row(norm_ffn[i]), ffn_wg, ffn_wu, ffn_wd, i, row(norm_final))
        hp = mix_ffn(mix_p, w_out, hp, *ffn_w, last)
        hs = mix_ffn(mix_s, w_out, hs, *ffn_w, last)

    y_prompt = hp.reshape(bp, t, d)[:, N_META:]
    y_sample = hs.reshape(bs, ls, d)
    st = lambda n: jnp.stack(outs[n])
    n_fox = kt_all.shape[0]
    to_heads = lambda a: jnp.transpose(a.reshape(n_fox, bp, FOX_HEADS, FOX_HEAD_DIM, t), (0, 1, 4, 2, 3))
    split_heads = lambda a: a.reshape(a.shape[:2] + (SSD_HEADS, SSD_HEAD_DIM, SSD_STATE))
    return (y_prompt, y_sample, to_heads(kt_all), to_heads(vt_all), st("lfp"), split_heads(ssm_p),
            st("conv_p"), st("ks"), st("vs"), st("lfs"), split_heads(ssm_s), st("conv_s"))
```
